```python
import math
import jax
import jax.numpy as jnp
from jax import lax
import numpy as np

D_MODEL = 2048
BATCH = 4
SEQ = 2048
DEPTH = 2
DEC_BATCH = 32
DEC_SEQ = 64
PAST_LEN = 2048

CHUNK = 64
DN_HEADS = 16
DN_DK = 128
DN_DV = 128
DN_QK_W = DN_HEADS * DN_DK
DN_V_W = DN_HEADS * DN_DV
DN_QKV_W = 2 * DN_QK_W + DN_V_W
CONV_W = 4
DN_SCALE = DN_DK ** -0.5
GMLP_CHUNK = 128
GMLP_GROUPS = 16
GMLP_GROUP_DIM = 128
GMLP_W = GMLP_GROUPS * GMLP_GROUP_DIM
OFF_Z = DN_QKV_W
OFF_B = OFF_Z + DN_V_W
OFF_A = OFF_B + DN_HEADS
OFF_U = OFF_A + DN_HEADS
OFF_V = OFF_U + GMLP_W
OFF_GA = OFF_V + GMLP_W
OFF_GB = OFF_GA + D_MODEL
IN_W = OFF_GB + D_MODEL
D_FF = 11 * D_MODEL // 4
N_EXPERTS = 8
TOP_K = 2
D_FF_EXPERT = D_FF // 2
N_DENSE = (DEPTH + 1) // 2
N_MOE = DEPTH // 2
RMS_EPS = 1e-6
LN_EPS = 1e-5
L2_EPS = 1e-6

kernel_name = 'hybrid_deltanet_gmlp_stream_step'


def rms_norm(x, g):
    xf = x.astype(jnp.float32)
    y = xf * lax.rsqrt(jnp.mean(xf * xf, -1, keepdims=True) + RMS_EPS)
    return (y * g.astype(jnp.float32)).astype(x.dtype)


def layer_norm(x, g, b):
    xf = x.astype(jnp.float32)
    xc = xf - jnp.mean(xf, -1, keepdims=True)
    var = jnp.mean(xc * xc, -1, keepdims=True)
    y = xc * lax.rsqrt(var + LN_EPS) * g.astype(jnp.float32) + b.astype(jnp.float32)
    return y.astype(x.dtype)


def l2_normalize(x):
    xf = x.astype(jnp.float32)
    return xf * lax.rsqrt(jnp.sum(xf * xf, -1, keepdims=True) + L2_EPS)


def causal_conv_silu(x, prev, w):
    xp = jnp.concatenate([prev.astype(x.dtype), x], axis=1)
    y = lax.conv_general_dilated(xp, w[:, None, :].astype(x.dtype), window_strides=(1,), padding='VALID',
                                 dimension_numbers=('NWC', 'WIO', 'NWC'), feature_group_count=x.shape[-1])
    return jax.nn.silu(y), xp[:, -(CONV_W - 1):]


def gated_delta_rule(q, k, v, beta, g, s0, chunk):
    B, T, H, DK = q.shape
    DV = v.shape[-1]
    N = T // chunk

    def blk(a):
        a = a.reshape((B, N, chunk, H) + a.shape[3:])
        return jnp.moveaxis(a, (1, 3), (0, 2))

    q, k, v, beta, g = blk(q), blk(k), blk(v), blk(beta), blk(g)
    gc = jnp.cumsum(g, axis=-1)
    idx = jnp.arange(chunk)
    causal = idx[:, None] >= idx[None, :]
    strict = idx[:, None] > idx[None, :]
    decay = jnp.exp(jnp.where(causal, gc[..., :, None] - gc[..., None, :], -jnp.inf))
    kb = k * beta[..., None]
    lower = jnp.where(strict, jnp.einsum('nbhid,nbhjd->nbhij', kb, k) * decay, 0.0)
    a_mat = lower + jnp.eye(chunk, dtype=jnp.float32)
    u = lax.linalg.triangular_solve(a_mat, v * beta[..., None], left_side=True, lower=True, unit_diagonal=True)
    w = lax.linalg.triangular_solve(a_mat, kb * jnp.exp(gc)[..., None], left_side=True, lower=True,
                                    unit_diagonal=True)
    qk = jnp.einsum('nbhid,nbhjd->nbhij', q, k) * decay
    qg = q * jnp.exp(gc)[..., None]
    glast = gc[..., -1]
    kd = k * jnp.exp(glast[..., None] - gc)[..., None]

    def step(s, inp):
        u_c, w_c, qk_c, qg_c, kd_c, gl_c = inp
        v_new = u_c - jnp.einsum('bhck,bhkv->bhcv', w_c, s)
        o = jnp.einsum('bhck,bhkv->bhcv', qg_c, s) + jnp.einsum('bhij,bhjv->bhiv', qk_c, v_new)
        s = s * jnp.exp(gl_c)[..., None, None] + jnp.einsum('bhck,bhcv->bhkv', kd_c, v_new)
        return s, o

    s, o = lax.scan(step, s0, (u, w, qk, qg, kd, glast))
    o = jnp.moveaxis(o, (0, 2), (1, 3)).reshape(B, T, H, DV)
    return o, s


def hybrid_mixer(h, conv_prev, s_prev, w_in, conv_w, a_log, dt_bias, dn_norm_g, gm_ln_g, gm_ln_b,
                 sp_w, sp_b, w_dn_out, w_gm_out, w_out):
    B, T, _ = h.shape
    p = h @ w_in
    qkv, z, b_lin, a_lin, gu, gv, gate_a, gate_b = jnp.split(
        p, [OFF_Z, OFF_B, OFF_A, OFF_U, OFF_V, OFF_GA, OFF_GB], axis=-1)

    qkv_c, conv_new = causal_conv_silu(qkv, conv_prev, conv_w)
    q, k, v = jnp.split(qkv_c, [DN_QK_W, 2 * DN_QK_W], axis=-1)
    q = l2_normalize(q.reshape(B, T, DN_HEADS, DN_DK)) * DN_SCALE
    k = l2_normalize(k.reshape(B, T, DN_HEADS, DN_DK))
    v = v.reshape(B, T, DN_HEADS, DN_DV).astype(jnp.float32)
    beta = jax.nn.sigmoid(b_lin.astype(jnp.float32))
    g = -jnp.exp(a_log.astype(jnp.float32)) * jax.nn.softplus(a_lin.astype(jnp.float32) + dt_bias.astype(jnp.float32))
    o, s_new = gated_delta_rule(q, k, v, beta, g, s_prev.astype(jnp.float32), min(T, CHUNK))
    o = rms_norm(o, dn_norm_g) * jax.nn.silu(z.reshape(B, T, DN_HEADS, DN_DV).astype(jnp.float32))
    o = o.reshape(B, T, DN_V_W).astype(h.dtype)

    gu = jax.nn.gelu(gu)
    vn = layer_norm(jax.nn.gelu(gv), gm_ln_g, gm_ln_b)
    L = min(T, GMLP_CHUNK)
    M = T // L
    vb = vn.reshape(B, M, L, GMLP_GROUPS, GMLP_GROUP_DIM)
    wl = jnp.tril(sp_w[:, :L, :L])
    mixed = jnp.einsum('gij,bmjgc->bmigc', wl, vb) + sp_b[:, :L].T[None, None, :, :, None]
    gm = (gu.reshape(B, M, L, GMLP_GROUPS, GMLP_GROUP_DIM) * mixed).reshape(B, T, GMLP_W)

    y = jax.nn.sigmoid(gate_a) * (o @ w_dn_out) + jax.nn.sigmoid(gate_b) * (gm @ w_gm_out)
    return y @ w_out, conv_new, s_new, vn


def swiglu(x, wg, wu, wd):
    return (jax.nn.silu(x @ wg) * (x @ wu)) @ wd


def moe_ffn(x, router_w, router_b, wg, wu, wd):
    B, T, D = x.shape
    xt = x.reshape(B * T, D)
    logits = xt.astype(jnp.float32) @ router_w.astype(jnp.float32) + router_b.astype(jnp.float32)
    top_v, top_i = lax.top_k(logits, TOP_K)
    probs = jax.nn.softmax(top_v, axis=-1)
    gates = jnp.sum(jax.nn.one_hot(top_i, N_EXPERTS, dtype=jnp.float32) * probs[..., None], axis=1)
    y = jnp.zeros_like(xt)
    for e in range(N_EXPERTS):
        y = y + gates[:, e:e + 1].astype(x.dtype) * swiglu(xt, wg[e], wu[e], wd[e])
    return y.reshape(B, T, D)


def trunk(x, conv_prev, s_prev, norm_mix_g, w_in, conv_w, a_log, dt_bias, dn_norm_g, gm_ln_g, gm_ln_b,
          sp_w, sp_b, w_dn_out, w_gm_out, w_out, norm_ffn_g, ffn_wg, ffn_wu, ffn_wd, router_w, router_b,
          moe_wg, moe_wu, moe_wd, final_g):
    conv_out, s_out, v_out = [], [], []
    for l in range(DEPTH):
        y, c_new, s_new, vn = hybrid_mixer(rms_norm(x, norm_mix_g[l]), conv_prev[l], s_prev[l], w_in[l],
                                           conv_w[l], a_log[l], dt_bias[l], dn_norm_g[l], gm_ln_g[l],
                                           gm_ln_b[l], sp_w[l], sp_b[l], w_dn_out[l], w_gm_out[l], w_out[l])
        x = x + y
        hf = rms_norm(x, norm_ffn_g[l])
        if l % 2 == 0:
            x = x + swiglu(hf, ffn_wg[l // 2], ffn_wu[l // 2], ffn_wd[l // 2])
        else:
            x = x + moe_ffn(hf, router_w[l // 2], router_b[l // 2], moe_wg[l // 2], moe_wu[l // 2], moe_wd[l // 2])
        conv_out.append(c_new)
        s_out.append(s_new)
        v_out.append(vn)
    return rms_norm(x, final_g), jnp.stack(conv_out), jnp.stack(s_out), jnp.stack(v_out)


def setup_inputs(seed: int = 0) -> dict:
    key = jax.random.key(seed)
    ks = jax.random.split(key, 32)
    f32 = jnp.float32

    def nrm(k, shape, scale):
        return jax.random.normal(k, shape, f32) * scale

    dt = jnp.exp(jax.random.uniform(ks[6], (DEPTH, DN_HEADS), f32, math.log(1e-3), math.log(1e-1)))
    return {
        'x_prompt': nrm(ks[0], (BATCH, SEQ, D_MODEL), 1.0),
        'x_sample': nrm(ks[1], (DEC_BATCH, DEC_SEQ, D_MODEL), 1.0),
        'state_conv': nrm(ks[2], (DEPTH, DEC_BATCH, CONV_W - 1, DN_QKV_W), 1.0),
        'state_delta': nrm(ks[3], (DEPTH, DEC_BATCH, DN_HEADS, DN_DK, DN_DV), 0.05),
        'norm_mix_g': 1.0 + nrm(ks[4], (DEPTH, D_MODEL), 0.02),
        'w_in': nrm(ks[5], (DEPTH, D_MODEL, IN_W), D_MODEL ** -0.5),
        'conv_w': nrm(ks[7], (DEPTH, CONV_W, DN_QKV_W), CONV_W ** -0.5),
        'a_log': jnp.log(jax.random.uniform(ks[8], (DEPTH, DN_HEADS), f32, 1.0, 16.0)),
        'dt_bias': dt + jnp.log(-jnp.expm1(-dt)),
        'dn_norm_g': 1.0 + nrm(ks[9], (DEPTH, DN_DV), 0.02),
        'gm_ln_g': 1.0 + nrm(ks[10], (DEPTH, GMLP_W), 0.02),
        'gm_ln_b': nrm(ks[11], (DEPTH, GMLP_W), 0.02),
        'sp_w': nrm(ks[12], (DEPTH, GMLP_GROUPS, GMLP_CHUNK, GMLP_CHUNK), GMLP_CHUNK ** -0.5),
        'sp_b': 1.0 + nrm(ks[13], (DEPTH, GMLP_GROUPS, GMLP_CHUNK), 0.02),
        'w_dn_out': nrm(ks[14], (DEPTH, DN_V_W, D_MODEL), DN_V_W ** -0.5),
        'w_gm_out': nrm(ks[15], (DEPTH, GMLP_W, D_MODEL), GMLP_W ** -0.5),
        'w_out': nrm(ks[16], (DEPTH, D_MODEL, D_MODEL), D_MODEL ** -0.5),
        'norm_ffn_g': 1.0 + nrm(ks[17], (DEPTH, D_MODEL), 0.02),
        'ffn_wg': nrm(ks[18], (N_DENSE, D_MODEL, D_FF), D_MODEL ** -0.5),
        'ffn_wu': nrm(ks[19], (N_DENSE, D_MODEL, D_FF), D_MODEL ** -0.5),
        'ffn_wd': nrm(ks[20], (N_DENSE, D_FF, D_MODEL), D_FF ** -0.5),
        'router_w': nrm(ks[21], (N_MOE, D_MODEL, N_EXPERTS), D_MODEL ** -0.5),
        'router_b': nrm(ks[22], (N_MOE, N_EXPERTS), 0.01),
        'moe_wg': nrm(ks[23], (N_MOE, N_EXPERTS, D_MODEL, D_FF_EXPERT), D_MODEL ** -0.5),
        'moe_wu': nrm(ks[24], (N_MOE, N_EXPERTS, D_MODEL, D_FF_EXPERT), D_MODEL ** -0.5),
        'moe_wd': nrm(ks[25], (N_MOE, N_EXPERTS, D_FF_EXPERT, D_MODEL), D_FF_EXPERT ** -0.5),
        'final_g': 1.0 + nrm(ks[26], (D_MODEL,), 0.02),
    }


def reference(x_prompt, x_sample, state_conv, state_delta, norm_mix_g, w_in, conv_w, a_log, dt_bias, dn_norm_g,
              gm_ln_g, gm_ln_b, sp_w, sp_b, w_dn_out, w_gm_out, w_out, norm_ffn_g, ffn_wg, ffn_wu, ffn_wd,
              router_w, router_b, moe_wg, moe_wu, moe_wd, final_g):
    weights = (norm_mix_g, w_in, conv_w, a_log, dt_bias, dn_norm_g, gm_ln_g, gm_ln_b, sp_w, sp_b, w_dn_out,
               w_gm_out, w_out, norm_ffn_g, ffn_wg, ffn_wu, ffn_wd, router_w, router_b, moe_wg, moe_wu,
               moe_wd, final_g)
    bp = x_prompt.shape[0]
    conv0 = jnp.zeros((DEPTH, bp, CONV_W - 1, DN_QKV_W), x_prompt.dtype)
    s0 = jnp.zeros((DEPTH, bp, DN_HEADS, DN_DK, DN_DV), jnp.float32)
    y_prompt, conv_p, delta_p, _ = trunk(x_prompt, conv0, s0, *weights)
    y_sample, conv_s, delta_s, v_s = trunk(x_sample, state_conv, state_delta, *weights)
    return (y_prompt, y_sample, conv_p, delta_p.astype(x_prompt.dtype), conv_s,
            delta_s.astype(state_delta.dtype), v_s)
```

```python
import functools
import math

import jax
import jax.numpy as jnp
from jax import lax
from jax.experimental import pallas as pl
from jax.experimental.pallas import tpu as pltpu

F32 = jnp.float32
BF16 = jnp.bfloat16
HIGHEST = lax.Precision.HIGHEST

D_MODEL = 2048
DEPTH = 2
CHUNK = 64
DN_HEADS = 16
DN_DK = 128
DN_DV = 128
DN_QK_W = DN_HEADS * DN_DK
DN_V_W = DN_HEADS * DN_DV
DN_QKV_W = 2 * DN_QK_W + DN_V_W
CONV_W = 4
DN_SCALE = DN_DK ** -0.5
GMLP_CHUNK = 128
GMLP_GROUPS = 16
GMLP_GROUP_DIM = 128
GMLP_W = GMLP_GROUPS * GMLP_GROUP_DIM
OFF_Z = DN_QKV_W
OFF_B = OFF_Z + DN_V_W
OFF_A = OFF_B + DN_HEADS
OFF_U = OFF_A + DN_HEADS
OFF_V = OFF_U + GMLP_W
OFF_GA = OFF_V + GMLP_W
OFF_GB = OFF_GA + D_MODEL
IN_W = OFF_GB + D_MODEL
D_FF = 11 * D_MODEL // 4
N_EXPERTS = 8
TOP_K = 2
D_FF_EXPERT = D_FF // 2
RMS_EPS = 1e-6
LN_EPS = 1e-5
L2_EPS = 1e-6

LANES = 128
SUBLANES = 8
V7X_VMEM_LIMIT = 56 * 1024 * 1024

MAIN_W = IN_W - 2 * DN_HEADS
COL_Z = OFF_Z
COL_U = COL_Z + DN_V_W
COL_V = COL_U + GMLP_W
COL_GA = COL_V + GMLP_W
COL_GB = COL_GA + D_MODEL

HG = 4
NHG = DN_HEADS // HG
HW = HG * DN_DK


def _tile(n, pref):
    t = pref
    while n % t:
        t //= 2
    return t


def _cparams(sem, vmem_bytes=V7X_VMEM_LIMIT):
    return pltpu.CompilerParams(dimension_semantics=sem, vmem_limit_bytes=vmem_bytes)


def _rms(x, g):
    ms = jnp.mean(x * x, axis=-1, keepdims=True)
    return x * lax.rsqrt(ms + RMS_EPS) * g


def _dot(a, b):
    return jnp.dot(a, b, preferred_element_type=F32)


def _dot_nt(a, b):
    return lax.dot_general(a, b, (((1,), (1,)), ((), ())), preferred_element_type=F32)


def _dot_tn(a, b):
    return lax.dot_general(a, b, (((0,), (0,)), ((), ())), preferred_element_type=F32)


def _silu(x):
    return x * jax.nn.sigmoid(x)


def _gelu(x):
    return 0.5 * x * (1.0 + jnp.tanh(math.sqrt(2.0 / math.pi) * (x + 0.044715 * (x * x * x))))


def _softplus(x):
    return jnp.maximum(x, 0.0) + jnp.log1p(jnp.exp(-jnp.abs(x)))


def _inproj_kernel(x_ref, g_ref, w_ref, wba_ref, p_ref, pba_ref, hn_ref):
    @pl.when(pl.program_id(1) == 0)
    def _():
        hn = _rms(x_ref[...], g_ref[...]).astype(BF16)
        hn_ref[...] = hn
        pba_ref[...] = _dot(hn, wba_ref[...])

    p_ref[...] = _dot(hn_ref[...], w_ref[...])


def _inproj(x, g, w_main, w_ba):
    n = x.shape[0]
    tm = _tile(n, 1024)
    tn = 1024
    return pl.pallas_call(
        _inproj_kernel,
        grid=(n // tm, MAIN_W // tn),
        in_specs=[
            pl.BlockSpec((tm, D_MODEL), lambda i, j: (i, 0)),
            pl.BlockSpec((1, D_MODEL), lambda i, j: (0, 0)),
            pl.BlockSpec((D_MODEL, tn), lambda i, j: (0, j)),
            pl.BlockSpec((D_MODEL, NHG * LANES), lambda i, j: (0, 0)),
        ],
        out_specs=[
            pl.BlockSpec((tm, tn), lambda i, j: (i, j)),
            pl.BlockSpec((tm, NHG * LANES), lambda i, j: (i, 0)),
        ],
        out_shape=[jax.ShapeDtypeStruct((n, MAIN_W), F32), jax.ShapeDtypeStruct((n, NHG * LANES), F32)],
        scratch_shapes=[pltpu.VMEM((tm, D_MODEL), BF16)],
        compiler_params=_cparams(("parallel", "arbitrary")),
        name="in_proj",
    )(x, g, w_main, w_ba)


def _delta_kernel(qr_ref, kr_ref, vr_ref, z_ref, ba_ref, cpq_ref, cpk_ref, cpv_ref, cwq_ref, cwk_ref, cwv_ref,
                  ab_ref, sp_ref, dng_ref, o_ref, sop_ref, sos_ref, xbuf_ref, st_ref, *, npb, ncp, ncs):
    s = pl.program_id(1)
    is_p = s < npb
    cidx = jnp.where(is_p, s % ncp, (s - npb) % ncs)
    first = cidx == 0
    last = cidx == jnp.where(is_p, ncp, ncs) - 1

    @pl.when(first)
    def _():
        for j, cp_ref in enumerate((cpq_ref, cpk_ref, cpv_ref)):
            xbuf_ref[j, SUBLANES - (CONV_W - 1):SUBLANES, :] = jnp.where(is_p, 0.0, cp_ref[0])
        st_ref[...] = jnp.where(is_p, 0.0, sp_ref[0])

    def conv_silu(j, raw_ref, cw_ref):
        raw = raw_ref[...]
        xbuf_ref[j, SUBLANES:SUBLANES + CHUNK, :] = raw
        acc = raw * cw_ref[CONV_W - 1:CONV_W, :]
        for sft in range(1, CONV_W):
            acc = acc + xbuf_ref[j, SUBLANES - sft:SUBLANES - sft + CHUNK, :] * cw_ref[CONV_W - 1 - sft:CONV_W - sft, :]
        xbuf_ref[j, 0:SUBLANES, :] = xbuf_ref[j, CHUNK:CHUNK + SUBLANES, :]
        return _silu(acc)

    qc = conv_silu(0, qr_ref, cwq_ref)
    kc = conv_silu(1, kr_ref, cwk_ref)
    vc = conv_silu(2, vr_ref, cwv_ref)

    ba = ba_ref[...]
    lane = lax.broadcasted_iota(jnp.int32, (CHUNK, LANES), 1)
    gval = -jnp.exp(ab_ref[0, 0:1, :]) * _softplus(ba + ab_ref[0, 1:2, :])
    bg = jnp.where(lane < HG, jax.nn.sigmoid(ba), gval)
    ri = lax.broadcasted_iota(jnp.int32, (CHUNK, CHUNK), 0)
    ci = lax.broadcasted_iota(jnp.int32, (CHUNK, CHUNK), 1)
    causal = ri >= ci
    strict = ri > ci
    gc_cols = jnp.dot(causal.astype(F32), bg, precision=HIGHEST, preferred_element_type=F32)
    gc_rows = gc_cols.T
    z = z_ref[...]
    dng = dng_ref[...]

    for h in range(HG):
        sl = slice(h * DN_DK, (h + 1) * DN_DK)
        beta = bg[:, h:h + 1]
        gcc = gc_cols[:, HG + h:HG + h + 1]
        gcr = gc_rows[HG + h:HG + h + 1, :]
        glast = gcc[CHUNK - 1:CHUNK, :]
        decay = jnp.exp(jnp.where(causal, gcc - gcr, -jnp.inf))
        egc = jnp.exp(gcc)

        qh = qc[:, sl]
        kh = kc[:, sl]
        q = qh * lax.rsqrt(jnp.sum(qh * qh, axis=-1, keepdims=True) + L2_EPS) * DN_SCALE
        k = kh * lax.rsqrt(jnp.sum(kh * kh, axis=-1, keepdims=True) + L2_EPS)
        v = vc[:, sl]
        kb = k * beta
        k16 = k.astype(BF16)

        nmat = jnp.where(strict, -(_dot_nt(kb.astype(BF16), k16) * decay), 0.0)
        r = nmat
        m = nmat
        for _ in range(5):
            m16 = m.astype(BF16)
            m = _dot(m16, m16)
            r = r + m + _dot(r.astype(BF16), m.astype(BF16))
        rhs = jnp.concatenate([v * beta, kb * egc], axis=1)
        uw = rhs + _dot(r.astype(BF16), rhs.astype(BF16))
        u = uw[:, :DN_DV]
        w = uw[:, DN_DV:]

        qk = _dot_nt(q.astype(BF16), k16) * decay
        st = st_ref[h]
        st16 = st.astype(BF16)
        ws = _dot(jnp.concatenate([w, q * egc], axis=0).astype(BF16), st16)
        v_new = u - ws[:CHUNK]
        v_new16 = v_new.astype(BF16)
        o = ws[CHUNK:] + _dot(qk.astype(BF16), v_new16)
        kd = k * jnp.exp(glast - gcc)
        st_ref[h] = st * jnp.exp(glast) + _dot_tn(kd.astype(BF16), v_new16)

        o_ref[:, sl] = (_rms(o, dng) * _silu(z[:, sl])).astype(o_ref.dtype)

    @pl.when(last & is_p)
    def _():
        sop_ref[0] = st_ref[...]

    @pl.when(last & jnp.logical_not(is_p))
    def _():
        sos_ref[0] = st_ref[...]


def _delta(p_main, p_ba, state_conv, state_delta, layer, conv_w, ab, dn_g, bp, tp, bs, ts):
    n = p_main.shape[0]
    ncp, ncs = tp // CHUNK, ts // CHUNK
    npb = bp * ncp
    nb = n // CHUNK

    def seq_s(s):
        return jnp.maximum(s - npb, 0) // ncs

    def seq_p(s):
        return jnp.minimum(s // ncp, bp - 1)

    def col(c):
        return pl.BlockSpec((CHUNK, HW), lambda g, s, c=c: (s, c * NHG + g))

    def cprev(c):
        return pl.BlockSpec((None, 1, CONV_W - 1, HW), lambda g, s, c=c: (layer, seq_s(s), 0, c * NHG + g))

    def cw(c):
        return pl.BlockSpec((CONV_W, HW), lambda g, s, c=c: (0, c * NHG + g))

    kern = functools.partial(_delta_kernel, npb=npb, ncp=ncp, ncs=ncs)
    return pl.pallas_call(
        kern,
        grid=(NHG, nb),
        in_specs=[
            col(0), col(1), col(2), col(3),
            pl.BlockSpec((CHUNK, LANES), lambda g, s: (s, g)),
            cprev(0), cprev(1), cprev(2),
            cw(0), cw(1), cw(2),
            pl.BlockSpec((1, SUBLANES, LANES), lambda g, s: (g, 0, 0)),
            pl.BlockSpec((None, 1, HG, DN_DK, DN_DV), lambda g, s: (layer, seq_s(s), g, 0, 0)),
            pl.BlockSpec((1, DN_DV), lambda g, s: (0, 0)),
        ],
        out_specs=[
            pl.BlockSpec((CHUNK, HW), lambda g, s: (s, g)),
            pl.BlockSpec((1, HG, DN_DK, DN_DV), lambda g, s: (seq_p(s), g, 0, 0)),
            pl.BlockSpec((1, HG, DN_DK, DN_DV), lambda g, s: (seq_s(s), g, 0, 0)),
        ],
        out_shape=[
            jax.ShapeDtypeStruct((n, DN_V_W), BF16),
            jax.ShapeDtypeStruct((bp, DN_HEADS, DN_DK, DN_DV), F32),
            jax.ShapeDtypeStruct((bs, DN_HEADS, DN_DK, DN_DV), F32),
        ],
        scratch_shapes=[
            pltpu.VMEM((3, SUBLANES + CHUNK, HW), F32),
            pltpu.VMEM((HG, DN_DK, DN_DV), F32),
        ],
        compiler_params=_cparams(("parallel", "arbitrary"), 32 * 1024 * 1024),
        name="delta",
    )(p_main, p_main, p_main, p_main, p_ba, state_conv, state_conv, state_conv, conv_w, conv_w, conv_w,
      ab, state_delta, dn_g)


def _gmlp_kernel(u_ref, v_ref, lg_ref, lb_ref, w_ref, b_ref, gm_ref, vn_ref, *, npb, sample_len):
    s = pl.program_id(0)
    is_s = s >= npb
    gu = _gelu(u_ref[...])
    gv = _gelu(v_ref[...])
    xc = gv - jnp.mean(gv, axis=-1, keepdims=True)
    var = jnp.mean(xc * xc, axis=-1, keepdims=True)
    vn = xc * lax.rsqrt(var + LN_EPS) * lg_ref[...] + lb_ref[...]

    @pl.when(is_s)
    def _():
        vn_ref[...] = vn

    ri = lax.broadcasted_iota(jnp.int32, (GMLP_CHUNK, GMLP_CHUNK), 0)
    ci = lax.broadcasted_iota(jnp.int32, (GMLP_CHUNK, GMLP_CHUNK), 1)
    same_seq = (ri // sample_len) == (ci // sample_len)
    mask = (ri >= ci) & (same_seq | jnp.logical_not(is_s))
    bias = b_ref[0]
    for g in range(GMLP_GROUPS):
        sl = slice(g * GMLP_GROUP_DIM, (g + 1) * GMLP_GROUP_DIM)
        wl = jnp.where(mask, w_ref[0, g], 0.0).astype(BF16)
        mixed = _dot(wl, vn[:, sl].astype(BF16)) + bias[:, g:g + 1]
        gm_ref[:, sl] = (gu[:, sl] * mixed).astype(gm_ref.dtype)


def _gmlp(p_main, ln_g, ln_b, w2, b2, n_prompt, sample_len):
    n = p_main.shape[0]
    nb = n // GMLP_CHUNK
    npb = n_prompt // GMLP_CHUNK
    ublk = COL_U // GMLP_W
    vblk = COL_V // GMLP_W
    kern = functools.partial(_gmlp_kernel, npb=npb, sample_len=sample_len)
    return pl.pallas_call(
        kern,
        grid=(nb,),
        in_specs=[
            pl.BlockSpec((GMLP_CHUNK, GMLP_W), lambda s: (s, ublk)),
            pl.BlockSpec((GMLP_CHUNK, GMLP_W), lambda s: (s, vblk)),
            pl.BlockSpec((1, GMLP_W), lambda s: (0, 0)),
            pl.BlockSpec((1, GMLP_W), lambda s: (0, 0)),
            pl.BlockSpec((1, GMLP_GROUPS, GMLP_CHUNK, GMLP_CHUNK), lambda s: (jnp.where(s >= npb, 1, 0), 0, 0, 0)),
            pl.BlockSpec((1, GMLP_CHUNK, LANES), lambda s: (jnp.where(s >= npb, 1, 0), 0, 0)),
        ],
        out_specs=[
            pl.BlockSpec((GMLP_CHUNK, GMLP_W), lambda s: (s, 0)),
            pl.BlockSpec((GMLP_CHUNK, GMLP_W), lambda s: (jnp.maximum(s - npb, 0), 0)),
        ],
        out_shape=[
            jax.ShapeDtypeStruct((n, GMLP_W), BF16),
            jax.ShapeDtypeStruct((n - n_prompt, GMLP_W), F32),
        ],
        compiler_params=_cparams(("arbitrary",), 32 * 1024 * 1024),
        name="gmlp",
    )(p_main, p_main, ln_g, ln_b, w2, b2)


def _merge_kernel(o_ref, gm_ref, wd_ref, wg_ref, ga_ref, gb_ref, y_ref):
    a = _dot(o_ref[...], wd_ref[...])
    b = _dot(gm_ref[...], wg_ref[...])
    y_ref[...] = (jax.nn.sigmoid(ga_ref[...]) * a + jax.nn.sigmoid(gb_ref[...]) * b).astype(y_ref.dtype)


def _merge(o, gm, w_dn, w_gm, p_main):
    n = o.shape[0]
    tm = _tile(n, 512)
    tn = 1024
    ga0, gb0 = COL_GA // tn, COL_GB // tn
    return pl.pallas_call(
        _merge_kernel,
        grid=(n // tm, D_MODEL // tn),
        in_specs=[
            pl.BlockSpec((tm, DN_V_W), lambda i, j: (i, 0)),
            pl.BlockSpec((tm, GMLP_W), lambda i, j: (i, 0)),
            pl.BlockSpec((DN_V_W, tn), lambda i, j: (0, j)),
            pl.BlockSpec((GMLP_W, tn), lambda i, j: (0, j)),
            pl.BlockSpec((tm, tn), lambda i, j: (i, ga0 + j)),
            pl.BlockSpec((tm, tn), lambda i, j: (i, gb0 + j)),
        ],
        out_specs=pl.BlockSpec((tm, tn), lambda i, j: (i, j)),
        out_shape=jax.ShapeDtypeStruct((n, D_MODEL), BF16),
        compiler_params=_cparams(("parallel", "parallel")),
        name="merge",
    )(o, gm, w_dn, w_gm, p_main, p_main)


def _outproj_kernel(y_ref, w_ref, x_ref, o_ref):
    o_ref[...] = x_ref[...] + _dot(y_ref[...], w_ref[...])


def _outproj(y, w, x):
    n = y.shape[0]
    tm = _tile(n, 512)
    tn = 1024
    return pl.pallas_call(
        _outproj_kernel,
        grid=(n // tm, D_MODEL // tn),
        in_specs=[
            pl.BlockSpec((tm, D_MODEL), lambda i, j: (i, 0)),
            pl.BlockSpec((D_MODEL, tn), lambda i, j: (0, j)),
            pl.BlockSpec((tm, tn), lambda i, j: (i, j)),
        ],
        out_specs=pl.BlockSpec((tm, tn), lambda i, j: (i, j)),
        out_shape=jax.ShapeDtypeStruct((n, D_MODEL), F32),
        compiler_params=_cparams(("parallel", "parallel")),
        name="out_proj",
    )(y, w, x)


def _ffn_kernel(x_ref, g_ref, wg_ref, wu_ref, wd_ref, o_ref, hn_ref):
    @pl.when(pl.program_id(1) == 0)
    def _():
        x = x_ref[...]
        hn_ref[...] = _rms(x, g_ref[...]).astype(BF16)
        o_ref[...] = x

    hn = hn_ref[...]
    hid = (_silu(_dot(hn, wg_ref[...])) * _dot(hn, wu_ref[...])).astype(BF16)
    o_ref[...] += _dot(hid, wd_ref[...])


def _ffn(x, g, wg, wu, wd):
    n = x.shape[0]
    tm = _tile(n, 512)
    tf = 512
    return pl.pallas_call(
        _ffn_kernel,
        grid=(n // tm, D_FF // tf),
        in_specs=[
            pl.BlockSpec((tm, D_MODEL), lambda i, j: (i, 0)),
            pl.BlockSpec((1, D_MODEL), lambda i, j: (0, 0)),
            pl.BlockSpec((D_MODEL, tf), lambda i, j: (0, j)),
            pl.BlockSpec((D_MODEL, tf), lambda i, j: (0, j)),
            pl.BlockSpec((tf, D_MODEL), lambda i, j: (j, 0)),
        ],
        out_specs=pl.BlockSpec((tm, D_MODEL), lambda i, j: (i, 0)),
        out_shape=jax.ShapeDtypeStruct((n, D_MODEL), F32),
        scratch_shapes=[pltpu.VMEM((tm, D_MODEL), BF16)],
        compiler_params=_cparams(("parallel", "arbitrary")),
        name="ffn_dense",
    )(x, g, wg, wu, wd)


def _router_kernel(x_ref, g_ref, rw_ref, rb_ref, meta_ref, cnt_ref, carry_ref):
    @pl.when(pl.program_id(0) == 0)
    def _():
        carry_ref[...] = jnp.zeros_like(carry_ref)

    tm = x_ref.shape[0]
    hn = _rms(x_ref[...], g_ref[...])
    logits = jnp.dot(hn, rw_ref[...], precision=HIGHEST, preferred_element_type=F32) + rb_ref[...]
    lane = lax.broadcasted_iota(jnp.int32, (tm, LANES), 1).astype(F32)
    m1 = jnp.max(logits, axis=-1, keepdims=True)
    i1 = jnp.min(jnp.where(logits == m1, lane, float(LANES)), axis=-1, keepdims=True)
    oh1 = lane == i1
    rest = jnp.where(oh1, -jnp.inf, logits)
    m2 = jnp.max(rest, axis=-1, keepdims=True)
    i2 = jnp.min(jnp.where(rest == m2, lane, float(LANES)), axis=-1, keepdims=True)
    oh2 = lane == i2
    e = jnp.exp(m2 - m1)
    p1 = 1.0 / (1.0 + e)
    p2 = e / (1.0 + e)

    onehot = jnp.where(oh1 | oh2, 1.0, 0.0)
    ri = lax.broadcasted_iota(jnp.int32, (tm, tm), 0)
    ci = lax.broadcasted_iota(jnp.int32, (tm, tm), 1)
    before = jnp.where(ri > ci, 1.0, 0.0).astype(BF16)
    rank = _dot(before, onehot.astype(BF16)) + carry_ref[...]
    r1 = jnp.sum(jnp.where(oh1, rank, 0.0), axis=-1, keepdims=True)
    r2 = jnp.sum(jnp.where(oh2, rank, 0.0), axis=-1, keepdims=True)
    carry_ref[...] += jnp.sum(onehot, axis=0, keepdims=True)

    meta = jnp.zeros((tm, LANES), F32)
    for idx, val in enumerate((i1, i2, p1, p2, r1, r2)):
        meta = jnp.where(lane == float(idx), val, meta)
    meta_ref[...] = meta
    cnt_ref[...] = carry_ref[...]


def _router(x, g, rw, rb):
    n = x.shape[0]
    tm = _tile(n, 256)
    return pl.pallas_call(
        _router_kernel,
        grid=(n // tm,),
        in_specs=[
            pl.BlockSpec((tm, D_MODEL), lambda i: (i, 0)),
            pl.BlockSpec((1, D_MODEL), lambda i: (0, 0)),
            pl.BlockSpec((D_MODEL, LANES), lambda i: (0, 0)),
            pl.BlockSpec((1, LANES), lambda i: (0, 0)),
        ],
        out_specs=[
            pl.BlockSpec((tm, LANES), lambda i: (i, 0)),
            pl.BlockSpec((1, LANES), lambda i: (0, 0)),
        ],
        out_shape=[jax.ShapeDtypeStruct((n, LANES), F32), jax.ShapeDtypeStruct((1, LANES), F32)],
        scratch_shapes=[pltpu.VMEM((1, LANES), F32)],
        compiler_params=_cparams(("arbitrary",), 32 * 1024 * 1024),
        name="moe_router",
    )(x, g, rw, rb)


PCH = 128


def _gather_kernel(idx_ref, src_ref, dst_ref, sem):
    n_out = dst_ref.shape[0]
    nch = n_out // PCH

    def row_copy(r):
        return pltpu.make_async_copy(src_ref.at[pl.ds(idx_ref[r], 1)], dst_ref.at[pl.ds(r, 1)], sem)

    def issue(c):
        def body(r, carry):
            row_copy(c * PCH + r).start()
            return carry
        lax.fori_loop(0, PCH, body, 0, unroll=8)

    def drain(c):
        def body(r, carry):
            row_copy(c * PCH + r).wait()
            return carry
        lax.fori_loop(0, PCH, body, 0, unroll=8)

    issue(0)

    def step(c, carry):
        issue(c)
        drain(c - 1)
        return carry

    lax.fori_loop(1, nch, step, 0)
    drain(nch - 1)


def _gather_rows(src, idx):
    n_out = idx.shape[0]
    assert n_out % PCH == 0
    return pl.pallas_call(
        _gather_kernel,
        grid_spec=pltpu.PrefetchScalarGridSpec(
            num_scalar_prefetch=1,
            grid=(1,),
            in_specs=[pl.BlockSpec(memory_space=pl.ANY)],
            out_specs=pl.BlockSpec(memory_space=pl.ANY),
            scratch_shapes=[pltpu.SemaphoreType.DMA(())],
        ),
        out_shape=jax.ShapeDtypeStruct((n_out, src.shape[1]), src.dtype),
        compiler_params=pltpu.CompilerParams(dimension_semantics=("arbitrary",)),
        name="gather_rows",
    )(idx, src)


def _expert_kernel(te_ref, na_ref, x_ref, g_ref, wg_ref, wu_ref, wd_ref, o_ref, hn_ref):
    i = pl.program_id(0)
    j = pl.program_id(1)
    active = i < na_ref[0]

    @pl.when(j == 0)
    def _():
        hn_ref[...] = _rms(x_ref[...], g_ref[...]).astype(BF16)
        o_ref[...] = jnp.zeros_like(o_ref)

    @pl.when(active)
    def _():
        hn = hn_ref[...]
        hid = (_silu(_dot(hn, wg_ref[0])) * _dot(hn, wu_ref[0])).astype(BF16)
        o_ref[...] += _dot(hid, wd_ref[0])


def _experts(xs, g, wg, wu, wd, tile_expert, n_active, te_rows):
    s_max = xs.shape[0]
    n_tiles = s_max // te_rows
    tf = 256
    nj = D_FF_EXPERT // tf

    def row(i, na):
        return jnp.minimum(i, na[0] - 1)

    def jj(i, j, na):
        return jnp.where(i < na[0], j, nj - 1)

    return pl.pallas_call(
        _expert_kernel,
        grid_spec=pltpu.PrefetchScalarGridSpec(
            num_scalar_prefetch=2,
            grid=(n_tiles, nj),
            in_specs=[
                pl.BlockSpec((te_rows, D_MODEL), lambda i, j, te, na: (row(i, na), 0)),
                pl.BlockSpec((1, D_MODEL), lambda i, j, te, na: (0, 0)),
                pl.BlockSpec((1, D_MODEL, tf), lambda i, j, te, na: (te[row(i, na)], 0, jj(i, j, na))),
                pl.BlockSpec((1, D_MODEL, tf), lambda i, j, te, na: (te[row(i, na)], 0, jj(i, j, na))),
                pl.BlockSpec((1, tf, D_MODEL), lambda i, j, te, na: (te[row(i, na)], jj(i, j, na), 0)),
            ],
            out_specs=pl.BlockSpec((te_rows, D_MODEL), lambda i, j, te, na: (i, 0)),
            scratch_shapes=[pltpu.VMEM((te_rows, D_MODEL), BF16)],
        ),
        out_shape=jax.ShapeDtypeStruct((s_max, D_MODEL), F32),
        compiler_params=_cparams(("arbitrary", "arbitrary")),
        name="moe_experts",
    )(tile_expert, n_active, xs, g, wg, wu, wd)


def _combine_kernel(x_ref, e1_ref, e2_ref, meta_ref, g_ref, y_ref):
    meta = meta_ref[...]
    xn = x_ref[...] + (meta[:, 2:3] * e1_ref[...] + meta[:, 3:4] * e2_ref[...])
    y_ref[...] = _rms(xn, g_ref[...])


def _combine(x, gathered, meta, g, row0, rows):
    n = x.shape[0]
    tm = _tile(math.gcd(math.gcd(row0, rows), n), 256)
    b0 = row0 // tm
    b2 = n // tm
    return pl.pallas_call(
        _combine_kernel,
        grid=(rows // tm,),
        in_specs=[
            pl.BlockSpec((tm, D_MODEL), lambda i: (b0 + i, 0)),
            pl.BlockSpec((tm, D_MODEL), lambda i: (b0 + i, 0)),
            pl.BlockSpec((tm, D_MODEL), lambda i: (b2 + b0 + i, 0)),
            pl.BlockSpec((tm, LANES), lambda i: (b0 + i, 0)),
            pl.BlockSpec((1, D_MODEL), lambda i: (0, 0)),
        ],
        out_specs=pl.BlockSpec((tm, D_MODEL), lambda i: (i, 0)),
        out_shape=jax.ShapeDtypeStruct((rows, D_MODEL), F32),
        compiler_params=_cparams(("parallel",), 32 * 1024 * 1024),
        name="moe_combine",
    )(x, gathered, gathered, meta, g)


EXPERT_TILE = 512


def _routing_tables(meta, counts, n):
    te = EXPERT_TILE
    i1 = meta[:, 0].astype(jnp.int32)
    i2 = meta[:, 1].astype(jnp.int32)
    r1 = meta[:, 4].astype(jnp.int32)
    r2 = meta[:, 5].astype(jnp.int32)
    cnt = counts[0, :N_EXPERTS].astype(jnp.int32)
    padded = (cnt + te - 1) // te * te
    ends = jnp.cumsum(padded)
    offs = ends - padded
    slot1 = offs[i1] + r1
    slot2 = offs[i2] + r2
    s_max = (TOP_K * n + N_EXPERTS * (te - 1) + te - 1) // te * te
    tok = jnp.arange(n, dtype=jnp.int32)
    src = jnp.zeros((s_max,), jnp.int32).at[slot1].set(tok).at[slot2].set(tok)
    n_active = (ends[-1] // te).astype(jnp.int32).reshape(1)
    starts = jnp.arange(s_max // te, dtype=jnp.int32) * te
    tile_expert = jnp.minimum(jnp.searchsorted(ends, starts, side="right"), N_EXPERTS - 1).astype(jnp.int32)
    return src, jnp.concatenate([slot1, slot2]), tile_expert, n_active


def _prep_in_weights(w_in):
    w_main = jnp.concatenate([w_in[:, :OFF_B], w_in[:, OFF_U:]], axis=1).astype(BF16)
    blocks = []
    for hg in range(NHG):
        blocks += [w_in[:, OFF_B + hg * HG:OFF_B + (hg + 1) * HG], w_in[:, OFF_A + hg * HG:OFF_A + (hg + 1) * HG],
                   jnp.zeros((D_MODEL, LANES - 2 * HG), w_in.dtype)]
    return w_main, jnp.concatenate(blocks, axis=1).astype(BF16)


def _prep_decay_params(a_log, dt_bias):
    rows = jnp.zeros((NHG, SUBLANES, LANES), F32)
    rows = rows.at[:, 0, HG:2 * HG].set(a_log.reshape(NHG, HG))
    rows = rows.at[:, 1, HG:2 * HG].set(dt_bias.reshape(NHG, HG))
    return rows


def _prep_gmlp_params(sp_w, sp_b, sample_len):
    reps = GMLP_CHUNK // sample_len
    w_s = jnp.tile(sp_w[:, :sample_len, :sample_len], (1, reps, reps))
    b_s = jnp.tile(sp_b[:, :sample_len], (1, reps))
    w2 = jnp.stack([sp_w, w_s])
    b2 = jnp.stack([sp_b.T, b_s.T])
    b2 = jnp.pad(b2, ((0, 0), (0, 0), (0, LANES - GMLP_GROUPS)))
    return w2, b2


def kernel(x_prompt, x_sample, state_conv, state_delta, norm_mix_g, w_in, conv_w, a_log, dt_bias, dn_norm_g,
           gm_ln_g, gm_ln_b, sp_w, sp_b, w_dn_out, w_gm_out, w_out, norm_ffn_g, ffn_wg, ffn_wu, ffn_wd,
           router_w, router_b, moe_wg, moe_wu, moe_wd, final_g):
    bp, tp, _ = x_prompt.shape
    bs, ts, _ = x_sample.shape
    n_p, n_s = bp * tp, bs * ts
    n = n_p + n_s
    assert tp % GMLP_CHUNK == 0 and GMLP_CHUNK % ts == 0 and n_s % GMLP_CHUNK == 0 and ts % CHUNK == 0

    x = jnp.concatenate([x_prompt.reshape(n_p, D_MODEL), x_sample.reshape(n_s, D_MODEL)], axis=0)
    conv_p, conv_s, delta_p, delta_s, v_s = [], [], [], [], []
    y_p = y_s = None
    for l in range(DEPTH):
        w_main, w_ba = _prep_in_weights(w_in[l])
        p_main, p_ba = _inproj(x, norm_mix_g[l].reshape(1, D_MODEL), w_main, w_ba)
        o, sp_out, ss_out = _delta(p_main, p_ba, state_conv, state_delta, l, conv_w[l],
                                   _prep_decay_params(a_log[l], dt_bias[l]), dn_norm_g[l].reshape(1, DN_DV),
                                   bp, tp, bs, ts)
        w2, b2 = _prep_gmlp_params(sp_w[l], sp_b[l], ts)
        gm, vn_s = _gmlp(p_main, gm_ln_g[l].reshape(1, GMLP_W), gm_ln_b[l].reshape(1, GMLP_W), w2, b2, n_p, ts)
        y = _merge(o, gm, w_dn_out[l].astype(BF16), w_gm_out[l].astype(BF16), p_main)
        x = _outproj(y, w_out[l].astype(BF16), x)

        conv_p.append(p_main[:n_p].reshape(bp, tp, MAIN_W)[:, tp - (CONV_W - 1):, :DN_QKV_W])
        conv_s.append(p_main[n_p:].reshape(bs, ts, MAIN_W)[:, ts - (CONV_W - 1):, :DN_QKV_W])
        delta_p.append(sp_out)
        delta_s.append(ss_out)
        v_s.append(vn_s.reshape(bs, ts, GMLP_W))

        g_ffn = norm_ffn_g[l].reshape(1, D_MODEL)
        if l % 2 == 0:
            x = _ffn(x, g_ffn, ffn_wg[l // 2].astype(BF16), ffn_wu[l // 2].astype(BF16), ffn_wd[l // 2].astype(BF16))
        else:
            e = l // 2
            rw = jnp.pad(router_w[e], ((0, 0), (0, LANES - N_EXPERTS)))
            rb = jnp.pad(router_b[e], (0, LANES - N_EXPERTS), constant_values=-jnp.inf).reshape(1, LANES)
            meta, counts = _router(x, g_ffn, rw, rb)
            src, back, tile_expert, n_active = _routing_tables(meta, counts, n)
            xs = _gather_rows(x, src)
            ys = _experts(xs, g_ffn, moe_wg[e].astype(BF16), moe_wu[e].astype(BF16), moe_wd[e].astype(BF16),
                          tile_expert, n_active, EXPERT_TILE)
            gathered = _gather_rows(ys, back)
            if l == DEPTH - 1:
                fg = final_g.reshape(1, D_MODEL)
                y_p = _combine(x, gathered, meta, fg, 0, n_p)
                y_s = _combine(x, gathered, meta, fg, n_p, n_s)
    return (y_p.reshape(bp, tp, D_MODEL), y_s.reshape(bs, ts, D_MODEL), jnp.stack(conv_p), jnp.stack(delta_p),
            jnp.stack(conv_s), jnp.stack(delta_s), jnp.stack(v_s))
```

```python
import functools
import math

import jax
import jax.numpy as jnp
from jax import lax
from jax.experimental import pallas as pl
from jax.experimental.pallas import tpu as pltpu

F32 = jnp.float32
BF16 = jnp.bfloat16
HIGHEST = lax.Precision.HIGHEST

D_MODEL = 2048
DEPTH = 2
CHUNK = 64
DN_HEADS = 16
DN_DK = 128
DN_DV = 128
DN_QK_W = DN_HEADS * DN_DK
DN_V_W = DN_HEADS * DN_DV
DN_QKV_W = 2 * DN_QK_W + DN_V_W
CONV_W = 4
DN_SCALE = DN_DK ** -0.5
GMLP_CHUNK = 128
GMLP_GROUPS = 16
GMLP_GROUP_DIM = 128
GMLP_W = GMLP_GROUPS * GMLP_GROUP_DIM
OFF_Z = DN_QKV_W
OFF_B = OFF_Z + DN_V_W
OFF_A = OFF_B + DN_HEADS
OFF_U = OFF_A + DN_HEADS
OFF_V = OFF_U + GMLP_W
OFF_GA = OFF_V + GMLP_W
OFF_GB = OFF_GA + D_MODEL
IN_W = OFF_GB + D_MODEL
D_FF = 11 * D_MODEL // 4
N_EXPERTS = 8
TOP_K = 2
D_FF_EXPERT = D_FF // 2
RMS_EPS = 1e-6
LN_EPS = 1e-5
L2_EPS = 1e-6

LANES = 128
SUBLANES = 8
V7X_VMEM_LIMIT = 56 * 1024 * 1024
SMALL_VMEM_LIMIT = 32 * 1024 * 1024

MAIN_W = IN_W - 2 * DN_HEADS
COL_Z = OFF_Z
COL_U = COL_Z + DN_V_W
COL_V = COL_U + GMLP_W
COL_GA = COL_V + GMLP_W
COL_GB = COL_GA + D_MODEL

ROW_CHUNKS = D_MODEL // LANES
N_PAIRS = DN_HEADS // 2
PAIR_W = 2 * DN_DK


def _tile(n, pref):
    t = pref
    while n % t:
        t //= 2
    return t


def _cparams(sem, vmem_bytes=V7X_VMEM_LIMIT):
    return pltpu.CompilerParams(dimension_semantics=sem, vmem_limit_bytes=vmem_bytes)


def _rms(x, g):
    ms = jnp.mean(x * x, axis=-1, keepdims=True)
    return x * lax.rsqrt(ms + RMS_EPS) * g


def _dot(a, b):
    return jnp.dot(a, b, preferred_element_type=F32)


def _dot_nt(a, b):
    return lax.dot_general(a, b, (((1,), (1,)), ((), ())), preferred_element_type=F32)


def _silu(x):
    return x * jax.nn.sigmoid(x)


def _gelu(x):
    return 0.5 * x * (1.0 + jnp.tanh(math.sqrt(2.0 / math.pi) * (x + 0.044715 * (x * x * x))))


def _softplus(x):
    return jnp.maximum(x, 0.0) + jnp.log1p(jnp.exp(-jnp.abs(x)))


def _inproj_kernel(x_ref, g_ref, w_ref, wba_ref, p_ref, pba_ref, hn_ref):
    @pl.when(pl.program_id(1) == 0)
    def _():
        hn = _rms(x_ref[...], g_ref[...]).astype(BF16)
        hn_ref[...] = hn
        pba_ref[...] = _dot(hn, wba_ref[...])

    p_ref[...] = _dot(hn_ref[...], w_ref[...])


def _inproj(x, g, w_main, w_ba):
    n = x.shape[0]
    tm = _tile(n, 1024)
    tn = 1024
    return pl.pallas_call(
        _inproj_kernel,
        grid=(n // tm, MAIN_W // tn),
        in_specs=[
            pl.BlockSpec((tm, D_MODEL), lambda i, j: (i, 0)),
            pl.BlockSpec((1, D_MODEL), lambda i, j: (0, 0)),
            pl.BlockSpec((D_MODEL, tn), lambda i, j: (0, j)),
            pl.BlockSpec((D_MODEL, LANES), lambda i, j: (0, 0)),
        ],
        out_specs=[
            pl.BlockSpec((tm, tn), lambda i, j: (i, j)),
            pl.BlockSpec((tm, LANES), lambda i, j: (i, 0)),
        ],
        out_shape=[jax.ShapeDtypeStruct((n, MAIN_W), F32), jax.ShapeDtypeStruct((n, LANES), F32)],
        scratch_shapes=[pltpu.VMEM((tm, D_MODEL), BF16)],
        compiler_params=_cparams(("parallel", "arbitrary")),
        name="in_proj",
    )(x, g, w_main, w_ba)


def _dpre_kernel(q_ref, k_ref, v_ref, pq_ref, pk_ref, pv_ref, cq_ref, ck_ref, cv_ref, wq_ref_, wk_ref_, wv_ref_,
                 ba_ref, ab_ref, u_ref, wqo_ref, lk_ref, eg_ref, xbuf_ref, *, npb, ncp, ncs):
    s = pl.program_id(0)
    is_p = s < npb
    first = jnp.where(is_p, s % ncp, (s - npb) % ncs) == 0
    hist = SUBLANES - (CONV_W - 1)

    def conv_silu(j, raw_ref, prev_ref, cp_ref, cw_ref):
        raw = raw_ref[...]
        xbuf_ref[j, 0:SUBLANES, :] = prev_ref[...]

        @pl.when(first)
        def _():
            xbuf_ref[j, hist:SUBLANES, :] = jnp.where(is_p, 0.0, cp_ref[0])

        xbuf_ref[j, SUBLANES:SUBLANES + CHUNK, :] = raw
        acc = raw * cw_ref[CONV_W - 1:CONV_W, :]
        for sft in range(1, CONV_W):
            acc = acc + xbuf_ref[j, SUBLANES - sft:SUBLANES - sft + CHUNK, :] * cw_ref[CONV_W - 1 - sft:CONV_W - sft, :]
        return _silu(acc)

    qc = conv_silu(0, q_ref, pq_ref, cq_ref, wq_ref_)
    kc = conv_silu(1, k_ref, pk_ref, ck_ref, wk_ref_)
    vc = conv_silu(2, v_ref, pv_ref, cv_ref, wv_ref_)

    ba = ba_ref[...]
    lane = lax.broadcasted_iota(jnp.int32, (CHUNK, LANES), 1)
    gval = -jnp.exp(ab_ref[0:1, :]) * _softplus(ba + ab_ref[1:2, :])
    bg = jnp.where(lane < DN_HEADS, jax.nn.sigmoid(ba), gval)
    r64 = lax.broadcasted_iota(jnp.int32, (CHUNK, CHUNK), 0)
    c64 = lax.broadcasted_iota(jnp.int32, (CHUNK, CHUNK), 1)
    gc_cols = jnp.dot((r64 >= c64).astype(F32), bg, precision=HIGHEST, preferred_element_type=F32)
    gc_rows = gc_cols.T
    gc_rows2 = jnp.concatenate([gc_rows, gc_rows], axis=1)

    ri = lax.broadcasted_iota(jnp.int32, (2 * CHUNK, 2 * CHUNK), 0)
    ci = lax.broadcasted_iota(jnp.int32, (2 * CHUNK, 2 * CHUNK), 1)
    same = (ri // CHUNK) == (ci // CHUNK)
    causal = same & (ri >= ci)
    strict = same & (ri > ci)
    left = lax.broadcasted_iota(jnp.int32, (1, 2 * CHUNK), 1) < CHUNK

    def stack(a, p):
        return jnp.concatenate([a[:, (2 * p) * DN_DK:(2 * p + 1) * DN_DK],
                                a[:, (2 * p + 1) * DN_DK:(2 * p + 2) * DN_DK]], axis=0)

    def colpair(a, off, p):
        return jnp.concatenate([a[:, off + 2 * p:off + 2 * p + 1], a[:, off + 2 * p + 1:off + 2 * p + 2]], axis=0)

    pairs = range(N_PAIRS)
    beta = [colpair(bg, 0, p) for p in pairs]
    gcc = [colpair(gc_cols, DN_HEADS, p) for p in pairs]
    gcr = [jnp.where(left, gc_rows2[DN_HEADS + 2 * p:DN_HEADS + 2 * p + 1, :],
                     gc_rows2[DN_HEADS + 2 * p + 1:DN_HEADS + 2 * p + 2, :]) for p in pairs]
    gl = [[gc_cols[CHUNK - 1:CHUNK, DN_HEADS + 2 * p + t:DN_HEADS + 2 * p + t + 1] for t in (0, 1)] for p in pairs]
    glr = [jnp.where(left, gl[p][0], gl[p][1]) for p in pairs]
    decay = [jnp.exp(jnp.where(causal, gcc[p] - gcr[p], -jnp.inf)) for p in pairs]
    egc = [jnp.exp(gcc[p]) for p in pairs]

    qh = [stack(qc, p) for p in pairs]
    kh = [stack(kc, p) for p in pairs]
    q = [qh[p] * lax.rsqrt(jnp.sum(qh[p] * qh[p], axis=-1, keepdims=True) + L2_EPS) * DN_SCALE for p in pairs]
    k = [kh[p] * lax.rsqrt(jnp.sum(kh[p] * kh[p], axis=-1, keepdims=True) + L2_EPS) for p in pairs]
    kb = [k[p] * beta[p] for p in pairs]
    k16 = [k[p].astype(BF16) for p in pairs]

    m = [jnp.where(strict, -(_dot_nt(kb[p].astype(BF16), k16[p]) * decay[p]), 0.0) for p in pairs]
    r = list(m)
    for _ in range(5):
        m16 = [m[p].astype(BF16) for p in pairs]
        m = [_dot(m16[p], m16[p]) for p in pairs]
        r = [r[p] + m[p] + _dot(r[p].astype(BF16), m[p].astype(BF16)) for p in pairs]
    rhs = [jnp.concatenate([stack(vc, p) * beta[p], kb[p] * egc[p]], axis=1) for p in pairs]
    uw = [rhs[p] + _dot(r[p].astype(BF16), rhs[p].astype(BF16)) for p in pairs]
    qk = [_dot_nt(q[p].astype(BF16), k16[p]) * decay[p] for p in pairs]
    kdt = [k[p].T * jnp.exp(glr[p] - gcr[p]) for p in pairs]

    for p in pairs:
        for t in (0, 1):
            sl = slice((2 * p + t) * DN_DK, (2 * p + t + 1) * DN_DK)
            rows = slice(t * CHUNK, (t + 1) * CHUNK)
            u_ref[:, sl] = uw[p][rows, :DN_DV]
            wqo_ref[0:CHUNK, sl] = uw[p][rows, DN_DV:].astype(BF16)
            wqo_ref[CHUNK:2 * CHUNK, sl] = (q[p][rows] * egc[p][rows]).astype(BF16)
            eg_ref[:, sl] = jnp.broadcast_to(jnp.exp(gl[p][t]), (SUBLANES, DN_DV))
        psl = slice(p * 2 * CHUNK, (p + 1) * 2 * CHUNK)
        lk_ref[0:CHUNK, psl] = jnp.where(left, qk[p][:CHUNK], qk[p][CHUNK:]).astype(BF16)
        lk_ref[CHUNK:3 * CHUNK, psl] = kdt[p].astype(BF16)


def _delta_pre(p_main, p_ba, state_conv, layer, conv_w, ab, bp, tp, bs, ts):
    n = p_main.shape[0]
    ncp, ncs = tp // CHUNK, ts // CHUNK
    npb = bp * ncp
    nb = n // CHUNK
    rows_per_blk = CHUNK // SUBLANES

    def seq_s(s):
        return jnp.maximum(s - npb, 0) // ncs

    def col(c):
        return pl.BlockSpec((CHUNK, DN_QK_W), lambda s, c=c: (s, c))

    def prev(c):
        return pl.BlockSpec((SUBLANES, DN_QK_W), lambda s, c=c: (jnp.maximum(s * rows_per_blk - 1, 0), c))

    def cprev(c):
        return pl.BlockSpec((None, 1, CONV_W - 1, DN_QK_W), lambda s, c=c: (layer, seq_s(s), 0, c))

    def cw(c):
        return pl.BlockSpec((CONV_W, DN_QK_W), lambda s, c=c: (0, c))

    kern = functools.partial(_dpre_kernel, npb=npb, ncp=ncp, ncs=ncs)
    return pl.pallas_call(
        kern,
        grid=(nb,),
        in_specs=[
            col(0), col(1), col(2), prev(0), prev(1), prev(2), cprev(0), cprev(1), cprev(2), cw(0), cw(1), cw(2),
            pl.BlockSpec((CHUNK, LANES), lambda s: (s, 0)),
            pl.BlockSpec((SUBLANES, LANES), lambda s: (0, 0)),
        ],
        out_specs=[
            pl.BlockSpec((CHUNK, DN_V_W), lambda s: (s, 0)),
            pl.BlockSpec((2 * CHUNK, DN_QK_W), lambda s: (s, 0)),
            pl.BlockSpec((3 * CHUNK, DN_HEADS * CHUNK), lambda s: (s, 0)),
            pl.BlockSpec((SUBLANES, DN_V_W), lambda s: (s, 0)),
        ],
        out_shape=[
            jax.ShapeDtypeStruct((n, DN_V_W), F32),
            jax.ShapeDtypeStruct((2 * n, DN_QK_W), BF16),
            jax.ShapeDtypeStruct((3 * n, DN_HEADS * CHUNK), BF16),
            jax.ShapeDtypeStruct((nb * SUBLANES, DN_V_W), F32),
        ],
        scratch_shapes=[pltpu.VMEM((3, SUBLANES + CHUNK, DN_QK_W), F32)],
        compiler_params=_cparams(("parallel",), SMALL_VMEM_LIMIT),
        name="delta_pre",
    )(p_main, p_main, p_main, p_main, p_main, p_main, state_conv, state_conv, state_conv, conv_w, conv_w, conv_w,
      p_ba, ab)


def _drec_kernel(u_ref, wq_ref, lk_ref, eg_ref, z_ref, sp_ref, dng_ref, o_ref, sop_ref, sos_ref, st_ref,
                 *, npb, ncp, ncs):
    s = pl.program_id(0)
    is_p = s < npb
    cidx = jnp.where(is_p, s % ncp, (s - npb) % ncs)
    first = cidx == 0
    last = cidx == jnp.where(is_p, ncp, ncs) - 1

    @pl.when(first)
    def _():
        for h in range(DN_HEADS):
            st_ref[:, h * DN_DV:(h + 1) * DN_DV] = jnp.where(is_p, 0.0, sp_ref[0, h])

    lane = lax.broadcasted_iota(jnp.int32, (1, PAIR_W), 1)
    left = lane < DN_DV
    dng = dng_ref[...]
    for p in range(N_PAIRS):
        psl = slice(p * PAIR_W, (p + 1) * PAIR_W)
        st = st_ref[:, psl]
        st16 = st.astype(BF16)
        zero = jnp.zeros_like(st16)
        sbd = jnp.concatenate([jnp.where(left, st16, zero), jnp.where(left, zero, st16)], axis=0)
        ws = _dot(wq_ref[:, psl], sbd)
        v_new = (u_ref[:, psl] - ws[:CHUNK]).astype(BF16)
        vzero = jnp.zeros_like(v_new)
        vbd = jnp.concatenate([jnp.where(left, v_new, vzero), jnp.where(left, vzero, v_new)], axis=0)
        t = _dot(lk_ref[:, p * 2 * CHUNK:(p + 1) * 2 * CHUNK], vbd)
        o = ws[CHUNK:] + t[:CHUNK]
        st_ref[:, psl] = st * eg_ref[0:1, psl] + t[CHUNK:]
        for hh in (0, 1):
            sl = slice(p * PAIR_W + hh * DN_DV, p * PAIR_W + (hh + 1) * DN_DV)
            oh = o[:, hh * DN_DV:(hh + 1) * DN_DV]
            o_ref[:, sl] = (_rms(oh, dng) * _silu(z_ref[:, sl])).astype(o_ref.dtype)

    @pl.when(last & is_p)
    def _():
        for h in range(DN_HEADS):
            sop_ref[0, h] = st_ref[:, h * DN_DV:(h + 1) * DN_DV]

    @pl.when(last & jnp.logical_not(is_p))
    def _():
        for h in range(DN_HEADS):
            sos_ref[0, h] = st_ref[:, h * DN_DV:(h + 1) * DN_DV]


def _delta_rec(u, wq, lk, eg, p_main, state_delta, layer, dn_g, bp, tp, bs, ts):
    n = u.shape[0]
    ncp, ncs = tp // CHUNK, ts // CHUNK
    npb = bp * ncp
    nb = n // CHUNK
    zblk = COL_Z // DN_V_W

    def seq_s(s):
        return jnp.maximum(s - npb, 0) // ncs

    def seq_p(s):
        return jnp.minimum(s // ncp, bp - 1)

    kern = functools.partial(_drec_kernel, npb=npb, ncp=ncp, ncs=ncs)
    return pl.pallas_call(
        kern,
        grid=(nb,),
        in_specs=[
            pl.BlockSpec((CHUNK, DN_V_W), lambda s: (s, 0)),
            pl.BlockSpec((2 * CHUNK, DN_QK_W), lambda s: (s, 0)),
            pl.BlockSpec((3 * CHUNK, DN_HEADS * CHUNK), lambda s: (s, 0)),
            pl.BlockSpec((SUBLANES, DN_V_W), lambda s: (s, 0)),
            pl.BlockSpec((CHUNK, DN_V_W), lambda s: (s, zblk)),
            pl.BlockSpec((None, 1, DN_HEADS, DN_DK, DN_DV), lambda s: (layer, seq_s(s), 0, 0, 0)),
            pl.BlockSpec((1, DN_DV), lambda s: (0, 0)),
        ],
        out_specs=[
            pl.BlockSpec((CHUNK, DN_V_W), lambda s: (s, 0)),
            pl.BlockSpec((1, DN_HEADS, DN_DK, DN_DV), lambda s: (seq_p(s), 0, 0, 0)),
            pl.BlockSpec((1, DN_HEADS, DN_DK, DN_DV), lambda s: (seq_s(s), 0, 0, 0)),
        ],
        out_shape=[
            jax.ShapeDtypeStruct((n, DN_V_W), BF16),
            jax.ShapeDtypeStruct((bp, DN_HEADS, DN_DK, DN_DV), F32),
            jax.ShapeDtypeStruct((bs, DN_HEADS, DN_DK, DN_DV), F32),
        ],
        scratch_shapes=[pltpu.VMEM((DN_DK, DN_V_W), F32)],
        compiler_params=_cparams(("arbitrary",), SMALL_VMEM_LIMIT),
        name="delta_rec",
    )(u, wq, lk, eg, p_main, state_delta, dn_g)


def _gmlp_kernel(u_ref, v_ref, lg_ref, lb_ref, w_ref, b_ref, gm_ref, vn_ref, *, npb, sample_len):
    s = pl.program_id(0)
    is_s = s >= npb
    gu = _gelu(u_ref[...])
    gv = _gelu(v_ref[...])
    xc = gv - jnp.mean(gv, axis=-1, keepdims=True)
    var = jnp.mean(xc * xc, axis=-1, keepdims=True)
    vn = xc * lax.rsqrt(var + LN_EPS) * lg_ref[...] + lb_ref[...]

    @pl.when(is_s)
    def _():
        vn_ref[...] = vn

    ri = lax.broadcasted_iota(jnp.int32, (GMLP_CHUNK, GMLP_CHUNK), 0)
    ci = lax.broadcasted_iota(jnp.int32, (GMLP_CHUNK, GMLP_CHUNK), 1)
    same_seq = (ri // sample_len) == (ci // sample_len)
    mask = (ri >= ci) & (same_seq | jnp.logical_not(is_s))
    bias = b_ref[0]
    for g in range(GMLP_GROUPS):
        sl = slice(g * GMLP_GROUP_DIM, (g + 1) * GMLP_GROUP_DIM)
        wl = jnp.where(mask, w_ref[0, g], 0.0).astype(BF16)
        mixed = _dot(wl, vn[:, sl].astype(BF16)) + bias[:, g:g + 1]
        gm_ref[:, sl] = (gu[:, sl] * mixed).astype(gm_ref.dtype)


def _gmlp(p_main, ln_g, ln_b, w2, b2, n_prompt, sample_len):
    n = p_main.shape[0]
    nb = n // GMLP_CHUNK
    npb = n_prompt // GMLP_CHUNK
    ublk = COL_U // GMLP_W
    vblk = COL_V // GMLP_W
    kern = functools.partial(_gmlp_kernel, npb=npb, sample_len=sample_len)
    return pl.pallas_call(
        kern,
        grid=(nb,),
        in_specs=[
            pl.BlockSpec((GMLP_CHUNK, GMLP_W), lambda s: (s, ublk)),
            pl.BlockSpec((GMLP_CHUNK, GMLP_W), lambda s: (s, vblk)),
            pl.BlockSpec((1, GMLP_W), lambda s: (0, 0)),
            pl.BlockSpec((1, GMLP_W), lambda s: (0, 0)),
            pl.BlockSpec((1, GMLP_GROUPS, GMLP_CHUNK, GMLP_CHUNK), lambda s: (jnp.where(s >= npb, 1, 0), 0, 0, 0)),
            pl.BlockSpec((1, GMLP_CHUNK, LANES), lambda s: (jnp.where(s >= npb, 1, 0), 0, 0)),
        ],
        out_specs=[
            pl.BlockSpec((GMLP_CHUNK, GMLP_W), lambda s: (s, 0)),
            pl.BlockSpec((GMLP_CHUNK, GMLP_W), lambda s: (jnp.maximum(s - npb, 0), 0)),
        ],
        out_shape=[
            jax.ShapeDtypeStruct((n, GMLP_W), BF16),
            jax.ShapeDtypeStruct((n - n_prompt, GMLP_W), F32),
        ],
        compiler_params=_cparams(("arbitrary",), SMALL_VMEM_LIMIT),
        name="gmlp",
    )(p_main, p_main, ln_g, ln_b, w2, b2)


def _merge_kernel(o_ref, gm_ref, wd_ref, wg_ref, ga_ref, gb_ref, y_ref):
    a = _dot(o_ref[...], wd_ref[...])
    b = _dot(gm_ref[...], wg_ref[...])
    y_ref[...] = (jax.nn.sigmoid(ga_ref[...]) * a + jax.nn.sigmoid(gb_ref[...]) * b).astype(y_ref.dtype)


def _merge(o, gm, w_dn, w_gm, p_main):
    n = o.shape[0]
    tm = _tile(n, 512)
    tn = 1024
    ga0, gb0 = COL_GA // tn, COL_GB // tn
    return pl.pallas_call(
        _merge_kernel,
        grid=(n // tm, D_MODEL // tn),
        in_specs=[
            pl.BlockSpec((tm, DN_V_W), lambda i, j: (i, 0)),
            pl.BlockSpec((tm, GMLP_W), lambda i, j: (i, 0)),
            pl.BlockSpec((DN_V_W, tn), lambda i, j: (0, j)),
            pl.BlockSpec((GMLP_W, tn), lambda i, j: (0, j)),
            pl.BlockSpec((tm, tn), lambda i, j: (i, ga0 + j)),
            pl.BlockSpec((tm, tn), lambda i, j: (i, gb0 + j)),
        ],
        out_specs=pl.BlockSpec((tm, tn), lambda i, j: (i, j)),
        out_shape=jax.ShapeDtypeStruct((n, D_MODEL), BF16),
        compiler_params=_cparams(("parallel", "parallel")),
        name="merge",
    )(o, gm, w_dn, w_gm, p_main, p_main)


def _outproj_kernel(y_ref, w_ref, x_ref, o_ref):
    o_ref[...] = x_ref[...] + _dot(y_ref[...], w_ref[...])


def _outproj(y, w, x):
    n = y.shape[0]
    tm = _tile(n, 512)
    tn = 1024
    return pl.pallas_call(
        _outproj_kernel,
        grid=(n // tm, D_MODEL // tn),
        in_specs=[
            pl.BlockSpec((tm, D_MODEL), lambda i, j: (i, 0)),
            pl.BlockSpec((D_MODEL, tn), lambda i, j: (0, j)),
            pl.BlockSpec((tm, tn), lambda i, j: (i, j)),
        ],
        out_specs=pl.BlockSpec((tm, tn), lambda i, j: (i, j)),
        out_shape=jax.ShapeDtypeStruct((n, D_MODEL), F32),
        compiler_params=_cparams(("parallel", "parallel")),
        name="out_proj",
    )(y, w, x)


def _ffn_kernel(x_ref, g_ref, wg_ref, wu_ref, wd_ref, o_ref, hn_ref):
    @pl.when(pl.program_id(1) == 0)
    def _():
        x = x_ref[...]
        hn_ref[...] = _rms(x, g_ref[...]).astype(BF16)
        o_ref[...] = x

    hn = hn_ref[...]
    hid = (_silu(_dot(hn, wg_ref[...])) * _dot(hn, wu_ref[...])).astype(BF16)
    o_ref[...] += _dot(hid, wd_ref[...])


def _ffn(x, g, wg, wu, wd):
    n = x.shape[0]
    tm = _tile(n, 512)
    tf = 512
    return pl.pallas_call(
        _ffn_kernel,
        grid=(n // tm, D_FF // tf),
        in_specs=[
            pl.BlockSpec((tm, D_MODEL), lambda i, j: (i, 0)),
            pl.BlockSpec((1, D_MODEL), lambda i, j: (0, 0)),
            pl.BlockSpec((D_MODEL, tf), lambda i, j: (0, j)),
            pl.BlockSpec((D_MODEL, tf), lambda i, j: (0, j)),
            pl.BlockSpec((tf, D_MODEL), lambda i, j: (j, 0)),
        ],
        out_specs=pl.BlockSpec((tm, D_MODEL), lambda i, j: (i, 0)),
        out_shape=jax.ShapeDtypeStruct((n, D_MODEL), F32),
        scratch_shapes=[pltpu.VMEM((tm, D_MODEL), BF16)],
        compiler_params=_cparams(("parallel", "arbitrary")),
        name="ffn_dense",
    )(x, g, wg, wu, wd)


def _router_kernel(x_ref, g_ref, rw_ref, rb_ref, meta_ref, cnt_ref, carry_ref):
    @pl.when(pl.program_id(0) == 0)
    def _():
        carry_ref[...] = jnp.zeros_like(carry_ref)

    tm = x_ref.shape[0]
    hn = _rms(x_ref[...], g_ref[...])
    logits = jnp.dot(hn, rw_ref[...], precision=HIGHEST, preferred_element_type=F32) + rb_ref[...]
    lane = lax.broadcasted_iota(jnp.int32, (tm, LANES), 1).astype(F32)
    m1 = jnp.max(logits, axis=-1, keepdims=True)
    i1 = jnp.min(jnp.where(logits == m1, lane, float(LANES)), axis=-1, keepdims=True)
    oh1 = lane == i1
    rest = jnp.where(oh1, -jnp.inf, logits)
    m2 = jnp.max(rest, axis=-1, keepdims=True)
    i2 = jnp.min(jnp.where(rest == m2, lane, float(LANES)), axis=-1, keepdims=True)
    oh2 = lane == i2
    e = jnp.exp(m2 - m1)
    p1 = 1.0 / (1.0 + e)
    p2 = e / (1.0 + e)

    onehot = jnp.where(oh1 | oh2, 1.0, 0.0)
    ri = lax.broadcasted_iota(jnp.int32, (tm, tm), 0)
    ci = lax.broadcasted_iota(jnp.int32, (tm, tm), 1)
    before = jnp.where(ri > ci, 1.0, 0.0).astype(BF16)
    rank = _dot(before, onehot.astype(BF16)) + carry_ref[...]
    r1 = jnp.sum(jnp.where(oh1, rank, 0.0), axis=-1, keepdims=True)
    r2 = jnp.sum(jnp.where(oh2, rank, 0.0), axis=-1, keepdims=True)
    carry_ref[...] += jnp.sum(onehot, axis=0, keepdims=True)

    meta = jnp.zeros((tm, LANES), F32)
    for idx, val in enumerate((i1, i2, p1, p2, r1, r2)):
        meta = jnp.where(lane == float(idx), val, meta)
    meta_ref[...] = meta
    cnt_ref[...] = carry_ref[...]


def _router(x, g, rw, rb):
    n = x.shape[0]
    tm = _tile(n, 256)
    return pl.pallas_call(
        _router_kernel,
        grid=(n // tm,),
        in_specs=[
            pl.BlockSpec((tm, D_MODEL), lambda i: (i, 0)),
            pl.BlockSpec((1, D_MODEL), lambda i: (0, 0)),
            pl.BlockSpec((D_MODEL, LANES), lambda i: (0, 0)),
            pl.BlockSpec((1, LANES), lambda i: (0, 0)),
        ],
        out_specs=[
            pl.BlockSpec((tm, LANES), lambda i: (i, 0)),
            pl.BlockSpec((1, LANES), lambda i: (0, 0)),
        ],
        out_shape=[jax.ShapeDtypeStruct((n, LANES), F32), jax.ShapeDtypeStruct((1, LANES), F32)],
        scratch_shapes=[pltpu.VMEM((1, LANES), F32)],
        compiler_params=_cparams(("arbitrary",), SMALL_VMEM_LIMIT),
        name="moe_router",
    )(x, g, rw, rb)


DMA_UNROLL = 8


def _gather_tile(idx_ref, idx_base, rows, src_ref, dst_ref, dst_base, sem, *, wait):
    def body(r, carry):
        cp = pltpu.make_async_copy(src_ref.at[idx_ref[idx_base + r]], dst_ref.at[dst_base + r], sem)
        if wait:
            cp.wait()
        else:
            cp.start()
        return carry
    lax.fori_loop(0, rows, body, 0, unroll=DMA_UNROLL)


def _expert_kernel(src_ref, te_ref, na_ref, x3_ref, g_ref, wg_ref, wu_ref, wd_ref, o_ref, buf_ref, hn_ref, acc_ref,
                   sem):
    i = pl.program_id(0)
    j = pl.program_id(1)
    nj = pl.num_programs(1)
    rows = hn_ref.shape[0]
    n_active = na_ref[0]
    active = i < n_active
    slot = i % 2

    @pl.when(j == 0)
    def _():
        acc_ref[...] = jnp.zeros_like(acc_ref)

    @pl.when((j == 0) & active)
    def _():
        @pl.when(i == 0)
        def _():
            _gather_tile(src_ref, 0, rows, x3_ref, buf_ref.at[0], 0, sem.at[0], wait=False)

        @pl.when(i + 1 < n_active)
        def _():
            _gather_tile(src_ref, (i + 1) * rows, rows, x3_ref, buf_ref.at[1 - slot], 0, sem.at[1 - slot], wait=False)

        _gather_tile(src_ref, i * rows, rows, x3_ref, buf_ref.at[slot], 0, sem.at[slot], wait=True)

        ss = jnp.zeros((rows, 1), F32)
        for c in range(ROW_CHUNKS):
            xc = buf_ref[slot, :, c, :]
            ss = ss + jnp.sum(xc * xc, axis=-1, keepdims=True)
        scale = lax.rsqrt(ss / D_MODEL + RMS_EPS)
        for c in range(ROW_CHUNKS):
            csl = slice(c * LANES, (c + 1) * LANES)
            hn_ref[:, csl] = (buf_ref[slot, :, c, :] * scale * g_ref[:, csl]).astype(BF16)

    @pl.when(active)
    def _():
        hn = hn_ref[...]
        hid = (_silu(_dot(hn, wg_ref[0])) * _dot(hn, wu_ref[0])).astype(BF16)
        acc_ref[...] += _dot(hid, wd_ref[0])

    @pl.when(j == nj - 1)
    def _():
        for c in range(ROW_CHUNKS):
            o_ref[:, c, :] = acc_ref[:, c * LANES:(c + 1) * LANES]


def _experts(x3, src, g, wg, wu, wd, tile_expert, n_active, te_rows):
    s_max = src.shape[0]
    n_tiles = s_max // te_rows
    tf = 256
    nj = D_FF_EXPERT // tf

    def row(i, na):
        return jnp.minimum(i, na[0] - 1)

    def jj(i, j, na):
        return jnp.where(i < na[0], j, nj - 1)

    return pl.pallas_call(
        _expert_kernel,
        grid_spec=pltpu.PrefetchScalarGridSpec(
            num_scalar_prefetch=3,
            grid=(n_tiles, nj),
            in_specs=[
                pl.BlockSpec(memory_space=pl.ANY),
                pl.BlockSpec((1, D_MODEL), lambda i, j, sr, te, na: (0, 0)),
                pl.BlockSpec((1, D_MODEL, tf), lambda i, j, sr, te, na: (te[row(i, na)], 0, jj(i, j, na))),
                pl.BlockSpec((1, D_MODEL, tf), lambda i, j, sr, te, na: (te[row(i, na)], 0, jj(i, j, na))),
                pl.BlockSpec((1, tf, D_MODEL), lambda i, j, sr, te, na: (te[row(i, na)], jj(i, j, na), 0)),
            ],
            out_specs=pl.BlockSpec((te_rows, ROW_CHUNKS, LANES), lambda i, j, sr, te, na: (i, 0, 0)),
            scratch_shapes=[
                pltpu.VMEM((2, te_rows, ROW_CHUNKS, LANES), F32),
                pltpu.VMEM((te_rows, D_MODEL), BF16),
                pltpu.VMEM((te_rows, D_MODEL), F32),
                pltpu.SemaphoreType.DMA((2,)),
            ],
        ),
        out_shape=jax.ShapeDtypeStruct((s_max, ROW_CHUNKS, LANES), F32),
        compiler_params=_cparams(("arbitrary", "arbitrary")),
        name="moe_experts",
    )(src, tile_expert, n_active, x3, g, wg, wu, wd)


def _combine_kernel(back_ref, ys3_ref, x_ref, meta_ref, g_ref, y_ref, buf_ref, sem, *, tile0, n_tok):
    i = pl.program_id(0)
    nt = pl.num_programs(0)
    tm = x_ref.shape[0]
    slot = i % 2

    def gather(tile, slt, wait):
        for kk in range(TOP_K):
            _gather_tile(back_ref, kk * n_tok + (tile0 + tile) * tm, tm, ys3_ref, buf_ref.at[slt], kk * tm,
                         sem.at[slt], wait=wait)

    @pl.when(i == 0)
    def _():
        gather(0, 0, False)

    @pl.when(i + 1 < nt)
    def _():
        gather(i + 1, 1 - slot, False)

    gather(i, slot, True)

    meta = meta_ref[...]
    p1 = meta[:, 2:3]
    p2 = meta[:, 3:4]
    ss = jnp.zeros((tm, 1), F32)
    for c in range(ROW_CHUNKS):
        csl = slice(c * LANES, (c + 1) * LANES)
        xn = x_ref[:, csl] + (p1 * buf_ref[slot, 0:tm, c, :] + p2 * buf_ref[slot, tm:2 * tm, c, :])
        ss = ss + jnp.sum(xn * xn, axis=-1, keepdims=True)
        y_ref[:, csl] = xn
    y_ref[...] = y_ref[...] * lax.rsqrt(ss / D_MODEL + RMS_EPS) * g_ref[...]


def _combine(x, ys3, back, meta, g, row0, rows):
    n = x.shape[0]
    tm = _tile(math.gcd(math.gcd(row0, rows), n), 256)
    b0 = row0 // tm
    kern = functools.partial(_combine_kernel, tile0=b0, n_tok=n)
    return pl.pallas_call(
        kern,
        grid_spec=pltpu.PrefetchScalarGridSpec(
            num_scalar_prefetch=1,
            grid=(rows // tm,),
            in_specs=[
                pl.BlockSpec(memory_space=pl.ANY),
                pl.BlockSpec((tm, D_MODEL), lambda i, bk: (b0 + i, 0)),
                pl.BlockSpec((tm, LANES), lambda i, bk: (b0 + i, 0)),
                pl.BlockSpec((1, D_MODEL), lambda i, bk: (0, 0)),
            ],
            out_specs=pl.BlockSpec((tm, D_MODEL), lambda i, bk: (i, 0)),
            scratch_shapes=[
                pltpu.VMEM((2, TOP_K * tm, ROW_CHUNKS, LANES), F32),
                pltpu.SemaphoreType.DMA((2,)),
            ],
        ),
        out_shape=jax.ShapeDtypeStruct((rows, D_MODEL), F32),
        compiler_params=_cparams(("arbitrary",), SMALL_VMEM_LIMIT),
        name="moe_combine",
    )(back, ys3, x, meta, g)


EXPERT_TILE = 512


def _routing_tables(meta, counts, n):
    te = EXPERT_TILE
    i1 = meta[:, 0].astype(jnp.int32)
    i2 = meta[:, 1].astype(jnp.int32)
    r1 = meta[:, 4].astype(jnp.int32)
    r2 = meta[:, 5].astype(jnp.int32)
    cnt = counts[0, :N_EXPERTS].astype(jnp.int32)
    padded = (cnt + te - 1) // te * te
    ends = jnp.cumsum(padded)
    offs = ends - padded
    slot1 = offs[i1] + r1
    slot2 = offs[i2] + r2
    s_max = (TOP_K * n + N_EXPERTS * (te - 1) + te - 1) // te * te
    tok = jnp.arange(n, dtype=jnp.int32)
    src = jnp.zeros((s_max,), jnp.int32).at[slot1].set(tok).at[slot2].set(tok)
    n_active = (ends[-1] // te).astype(jnp.int32).reshape(1)
    starts = jnp.arange(s_max // te, dtype=jnp.int32) * te
    tile_expert = jnp.minimum(jnp.searchsorted(ends, starts, side="right"), N_EXPERTS - 1).astype(jnp.int32)
    return src, jnp.concatenate([slot1, slot2]), tile_expert, n_active


def _prep_in_weights(w_in):
    w_main = jnp.concatenate([w_in[:, :OFF_B], w_in[:, OFF_U:]], axis=1).astype(BF16)
    w_ba = jnp.pad(w_in[:, OFF_B:OFF_U], ((0, 0), (0, LANES - 2 * DN_HEADS))).astype(BF16)
    return w_main, w_ba


def _prep_decay_params(a_log, dt_bias):
    rows = jnp.zeros((SUBLANES, LANES), F32)
    rows = rows.at[0, DN_HEADS:2 * DN_HEADS].set(a_log)
    rows = rows.at[1, DN_HEADS:2 * DN_HEADS].set(dt_bias)
    return rows


def _prep_gmlp_params(sp_w, sp_b, sample_len):
    reps = GMLP_CHUNK // sample_len
    w_s = jnp.tile(sp_w[:, :sample_len, :sample_len], (1, reps, reps))
    b_s = jnp.tile(sp_b[:, :sample_len], (1, reps))
    w2 = jnp.stack([sp_w, w_s])
    b2 = jnp.stack([sp_b.T, b_s.T])
    b2 = jnp.pad(b2, ((0, 0), (0, 0), (0, LANES - GMLP_GROUPS)))
    return w2, b2


def _last_rows(p_main, row0, b, t):
    rows = (row0 + (jnp.arange(b, dtype=jnp.int32) * t + t - (CONV_W - 1))[:, None]
            + jnp.arange(CONV_W - 1, dtype=jnp.int32)[None, :])
    return jnp.take(p_main, rows.reshape(-1), axis=0)[:, :DN_QKV_W].reshape(b, CONV_W - 1, DN_QKV_W)


def kernel(x_prompt, x_sample, state_conv, state_delta, norm_mix_g, w_in, conv_w, a_log, dt_bias, dn_norm_g,
           gm_ln_g, gm_ln_b, sp_w, sp_b, w_dn_out, w_gm_out, w_out, norm_ffn_g, ffn_wg, ffn_wu, ffn_wd,
           router_w, router_b, moe_wg, moe_wu, moe_wd, final_g):
    bp, tp, _ = x_prompt.shape
    bs, ts, _ = x_sample.shape
    n_p, n_s = bp * tp, bs * ts
    n = n_p + n_s
    assert tp % GMLP_CHUNK == 0 and GMLP_CHUNK % ts == 0 and n_s % GMLP_CHUNK == 0 and ts % CHUNK == 0

    x = jnp.concatenate([x_prompt.reshape(n_p, D_MODEL), x_sample.reshape(n_s, D_MODEL)], axis=0)
    conv_p, conv_s, delta_p, delta_s, v_s = [], [], [], [], []
    y_p = y_s = None
    for l in range(DEPTH):
        w_main, w_ba = _prep_in_weights(w_in[l])
        p_main, p_ba = _inproj(x, norm_mix_g[l].reshape(1, D_MODEL), w_main, w_ba)
        u, wq, lk, eg = _delta_pre(p_main, p_ba, state_conv, l, conv_w[l], _prep_decay_params(a_log[l], dt_bias[l]),
                                   bp, tp, bs, ts)
        o, sp_out, ss_out = _delta_rec(u, wq, lk, eg, p_main, state_delta, l, dn_norm_g[l].reshape(1, DN_DV),
                                       bp, tp, bs, ts)
        w2, b2 = _prep_gmlp_params(sp_w[l], sp_b[l], ts)
        gm, vn_s = _gmlp(p_main, gm_ln_g[l].reshape(1, GMLP_W), gm_ln_b[l].reshape(1, GMLP_W), w2, b2, n_p, ts)
        y = _merge(o, gm, w_dn_out[l].astype(BF16), w_gm_out[l].astype(BF16), p_main)
        x = _outproj(y, w_out[l].astype(BF16), x)

        conv_p.append(_last_rows(p_main, 0, bp, tp))
        conv_s.append(_last_rows(p_main, n_p, bs, ts))
        delta_p.append(sp_out)
        delta_s.append(ss_out)
        v_s.append(vn_s.reshape(bs, ts, GMLP_W))

        g_ffn = norm_ffn_g[l].reshape(1, D_MODEL)
        if l % 2 == 0:
            x = _ffn(x, g_ffn, ffn_wg[l // 2].astype(BF16), ffn_wu[l // 2].astype(BF16), ffn_wd[l // 2].astype(BF16))
        else:
            e = l // 2
            rw = jnp.pad(router_w[e], ((0, 0), (0, LANES - N_EXPERTS)))
            rb = jnp.pad(router_b[e], (0, LANES - N_EXPERTS), constant_values=-jnp.inf).reshape(1, LANES)
            meta, counts = _router(x, g_ffn, rw, rb)
            src, back, tile_expert, n_active = _routing_tables(meta, counts, n)
            ys3 = _experts(x.reshape(n, ROW_CHUNKS, LANES), src, g_ffn, moe_wg[e].astype(BF16),
                           moe_wu[e].astype(BF16), moe_wd[e].astype(BF16), tile_expert, n_active, EXPERT_TILE)
            if l == DEPTH - 1:
                fg = final_g.reshape(1, D_MODEL)
                y_p = _combine(x, ys3, back, meta, fg, 0, n_p)
                y_s = _combine(x, ys3, back, meta, fg, n_p, n_s)
    return (y_p.reshape(bp, tp, D_MODEL), y_s.reshape(bs, ts, D_MODEL), jnp.stack(conv_p), jnp.stack(delta_p),
            jnp.stack(conv_s), jnp.stack(delta_s), jnp.stack(v_s))
```

```python
import functools
import math

import jax
import jax.numpy as jnp
from jax import lax
from jax.experimental import pallas as pl
from jax.experimental.pallas import tpu as pltpu

F32 = jnp.float32
BF16 = jnp.bfloat16
HIGHEST = lax.Precision.HIGHEST

D_MODEL = 2048
DEPTH = 2
CHUNK = 64
DN_HEADS = 16
DN_DK = 128
DN_DV = 128
DN_QK_W = DN_HEADS * DN_DK
DN_V_W = DN_HEADS * DN_DV
DN_QKV_W = 2 * DN_QK_W + DN_V_W
CONV_W = 4
DN_SCALE = DN_DK ** -0.5
GMLP_CHUNK = 128
GMLP_GROUPS = 16
GMLP_GROUP_DIM = 128
GMLP_W = GMLP_GROUPS * GMLP_GROUP_DIM
OFF_Z = DN_QKV_W
OFF_B = OFF_Z + DN_V_W
OFF_A = OFF_B + DN_HEADS
OFF_U = OFF_A + DN_HEADS
OFF_V = OFF_U + GMLP_W
OFF_GA = OFF_V + GMLP_W
OFF_GB = OFF_GA + D_MODEL
IN_W = OFF_GB + D_MODEL
D_FF = 11 * D_MODEL // 4
N_EXPERTS = 8
TOP_K = 2
D_FF_EXPERT = D_FF // 2
RMS_EPS = 1e-6
LN_EPS = 1e-5
L2_EPS = 1e-6

LANES = 128
SUBLANES = 8
V7X_VMEM_LIMIT = 56 * 1024 * 1024
SMALL_VMEM_LIMIT = 32 * 1024 * 1024

MAIN_W = IN_W - 2 * DN_HEADS
COL_Z = OFF_Z
COL_U = COL_Z + DN_V_W
COL_V = COL_U + GMLP_W
COL_GA = COL_V + GMLP_W
COL_GB = COL_GA + D_MODEL

ROW_CHUNKS = D_MODEL // LANES
N_PAIRS = DN_HEADS // 2
PAIR_W = 2 * DN_DK
PAIR_GROUP = 8


def _tile(n, pref):
    t = pref
    while n % t:
        t //= 2
    return t


def _cparams(sem, vmem_bytes=V7X_VMEM_LIMIT):
    return pltpu.CompilerParams(dimension_semantics=sem, vmem_limit_bytes=vmem_bytes)


def _rms(x, g):
    ms = jnp.mean(x * x, axis=-1, keepdims=True)
    return x * lax.rsqrt(ms + RMS_EPS) * g


def _dot(a, b):
    return jnp.dot(a, b, preferred_element_type=F32)


def _dot_nt(a, b):
    return lax.dot_general(a, b, (((1,), (1,)), ((), ())), preferred_element_type=F32)


def _silu(x):
    return x * jax.nn.sigmoid(x)


def _gelu(x):
    return 0.5 * x * (1.0 + jnp.tanh(math.sqrt(2.0 / math.pi) * (x + 0.044715 * (x * x * x))))


def _softplus(x):
    return jnp.maximum(x, 0.0) + jnp.log1p(jnp.exp(-jnp.abs(x)))


def _inproj_kernel(x_ref, g_ref, w_ref, wba_ref, p_ref, pba_ref, hn_ref):
    @pl.when(pl.program_id(1) == 0)
    def _():
        hn = _rms(x_ref[...], g_ref[...]).astype(BF16)
        hn_ref[...] = hn
        pba_ref[...] = _dot(hn, wba_ref[...])

    p_ref[...] = _dot(hn_ref[...], w_ref[...])


def _inproj(x, g, w_main, w_ba):
    n = x.shape[0]
    tm = _tile(n, 1024)
    tn = 1024
    return pl.pallas_call(
        _inproj_kernel,
        grid=(n // tm, MAIN_W // tn),
        in_specs=[
            pl.BlockSpec((tm, D_MODEL), lambda i, j: (i, 0)),
            pl.BlockSpec((1, D_MODEL), lambda i, j: (0, 0)),
            pl.BlockSpec((D_MODEL, tn), lambda i, j: (0, j)),
            pl.BlockSpec((D_MODEL, LANES), lambda i, j: (0, 0)),
        ],
        out_specs=[
            pl.BlockSpec((tm, tn), lambda i, j: (i, j)),
            pl.BlockSpec((tm, LANES), lambda i, j: (i, 0)),
        ],
        out_shape=[jax.ShapeDtypeStruct((n, MAIN_W), F32), jax.ShapeDtypeStruct((n, LANES), F32)],
        scratch_shapes=[pltpu.VMEM((tm, D_MODEL), BF16)],
        compiler_params=_cparams(("parallel", "arbitrary")),
        name="in_proj",
    )(x, g, w_main, w_ba)


def _dpre_kernel(q_ref, k_ref, v_ref, pq_ref, pk_ref, pv_ref, cq_ref, ck_ref, cv_ref, cwq_ref, cwk_ref, cwv_ref,
                 ba_ref, ab_ref, u_ref, wqo_ref, lk_ref, eg_ref, xbuf_ref, cbuf_ref, *, npb, ncp, ncs):
    s = pl.program_id(0)
    is_p = s < npb
    first = jnp.where(is_p, s % ncp, (s - npb) % ncs) == 0
    hist = SUBLANES - (CONV_W - 1)

    def conv_silu(j, raw_ref, prev_ref, cp_ref, cw_ref):
        raw = raw_ref[...]
        xbuf_ref[j, 0:SUBLANES, :] = prev_ref[...]

        @pl.when(first)
        def _():
            xbuf_ref[j, hist:SUBLANES, :] = jnp.where(is_p, 0.0, cp_ref[0])

        xbuf_ref[j, SUBLANES:SUBLANES + CHUNK, :] = raw
        acc = raw * cw_ref[CONV_W - 1:CONV_W, :]
        for sft in range(1, CONV_W):
            acc = acc + xbuf_ref[j, SUBLANES - sft:SUBLANES - sft + CHUNK, :] * cw_ref[CONV_W - 1 - sft:CONV_W - sft, :]
        return _silu(acc)

    cbuf_ref[0] = conv_silu(0, q_ref, pq_ref, cq_ref, cwq_ref)
    cbuf_ref[1] = conv_silu(1, k_ref, pk_ref, ck_ref, cwk_ref)
    cbuf_ref[2] = conv_silu(2, v_ref, pv_ref, cv_ref, cwv_ref)

    ba = ba_ref[...]
    lane = lax.broadcasted_iota(jnp.int32, (CHUNK, LANES), 1)
    gval = -jnp.exp(ab_ref[0:1, :]) * _softplus(ba + ab_ref[1:2, :])
    bg = jnp.where(lane < DN_HEADS, jax.nn.sigmoid(ba), gval)
    r64 = lax.broadcasted_iota(jnp.int32, (CHUNK, CHUNK), 0)
    c64 = lax.broadcasted_iota(jnp.int32, (CHUNK, CHUNK), 1)
    gc_cols = jnp.dot((r64 >= c64).astype(F32), bg, precision=HIGHEST, preferred_element_type=F32)
    gc_rows = gc_cols.T
    gc_rows2 = jnp.concatenate([gc_rows, gc_rows], axis=1)

    ri = lax.broadcasted_iota(jnp.int32, (2 * CHUNK, 2 * CHUNK), 0)
    ci = lax.broadcasted_iota(jnp.int32, (2 * CHUNK, 2 * CHUNK), 1)
    same = (ri // CHUNK) == (ci // CHUNK)
    causal = same & (ri >= ci)
    strict = same & (ri > ci)
    left = lax.broadcasted_iota(jnp.int32, (1, 2 * CHUNK), 1) < CHUNK

    def stack(j, p):
        return jnp.concatenate([cbuf_ref[j, :, (2 * p) * DN_DK:(2 * p + 1) * DN_DK],
                                cbuf_ref[j, :, (2 * p + 1) * DN_DK:(2 * p + 2) * DN_DK]], axis=0)

    def colpair(a, off, p):
        return jnp.concatenate([a[:, off + 2 * p:off + 2 * p + 1], a[:, off + 2 * p + 1:off + 2 * p + 2]], axis=0)

    def pair_group(pairs):
        beta = {p: colpair(bg, 0, p) for p in pairs}
        gcc = {p: colpair(gc_cols, DN_HEADS, p) for p in pairs}
        gcr = {p: jnp.where(left, gc_rows2[DN_HEADS + 2 * p:DN_HEADS + 2 * p + 1, :],
                            gc_rows2[DN_HEADS + 2 * p + 1:DN_HEADS + 2 * p + 2, :]) for p in pairs}
        gl = {p: [gc_cols[CHUNK - 1:CHUNK, DN_HEADS + 2 * p + t:DN_HEADS + 2 * p + t + 1] for t in (0, 1)]
              for p in pairs}
        glr = {p: jnp.where(left, gl[p][0], gl[p][1]) for p in pairs}
        decay = {p: jnp.exp(jnp.where(causal, gcc[p] - gcr[p], -jnp.inf)) for p in pairs}
        egc = {p: jnp.exp(gcc[p]) for p in pairs}

        qh = {p: stack(0, p) for p in pairs}
        kh = {p: stack(1, p) for p in pairs}
        q = {p: qh[p] * lax.rsqrt(jnp.sum(qh[p] * qh[p], axis=-1, keepdims=True) + L2_EPS) * DN_SCALE for p in pairs}
        k = {p: kh[p] * lax.rsqrt(jnp.sum(kh[p] * kh[p], axis=-1, keepdims=True) + L2_EPS) for p in pairs}
        kb = {p: k[p] * beta[p] for p in pairs}
        k16 = {p: k[p].astype(BF16) for p in pairs}

        m = {p: jnp.where(strict, -(_dot_nt(kb[p].astype(BF16), k16[p]) * decay[p]), 0.0) for p in pairs}
        r = dict(m)
        for _ in range(5):
            m16 = {p: m[p].astype(BF16) for p in pairs}
            m = {p: _dot(m16[p], m16[p]) for p in pairs}
            r = {p: r[p] + m[p] + _dot(r[p].astype(BF16), m[p].astype(BF16)) for p in pairs}
        rhs = {p: jnp.concatenate([stack(2, p) * beta[p], kb[p] * egc[p]], axis=1) for p in pairs}
        uw = {p: rhs[p] + _dot(r[p].astype(BF16), rhs[p].astype(BF16)) for p in pairs}
        qk = {p: _dot_nt(q[p].astype(BF16), k16[p]) * decay[p] for p in pairs}
        kdt = {p: k[p].T * jnp.exp(glr[p] - gcr[p]) for p in pairs}

        for p in pairs:
            for t in (0, 1):
                sl = slice((2 * p + t) * DN_DK, (2 * p + t + 1) * DN_DK)
                rows = slice(t * CHUNK, (t + 1) * CHUNK)
                u_ref[:, sl] = uw[p][rows, :DN_DV]
                wqo_ref[0:CHUNK, sl] = uw[p][rows, DN_DV:].astype(BF16)
                wqo_ref[CHUNK:2 * CHUNK, sl] = (q[p][rows] * egc[p][rows]).astype(BF16)
                eg_ref[:, sl] = jnp.broadcast_to(jnp.exp(gl[p][t]), (SUBLANES, DN_DV))
            psl = slice(p * 2 * CHUNK, (p + 1) * 2 * CHUNK)
            lk_ref[0:CHUNK, psl] = jnp.where(left, qk[p][:CHUNK], qk[p][CHUNK:]).astype(BF16)
            lk_ref[CHUNK:3 * CHUNK, psl] = kdt[p].astype(BF16)

    for first_pair in range(0, N_PAIRS, PAIR_GROUP):
        pair_group(range(first_pair, first_pair + PAIR_GROUP))


def _delta_pre(p_main, p_ba, state_conv, layer, conv_w, ab, bp, tp, bs, ts):
    n = p_main.shape[0]
    ncp, ncs = tp // CHUNK, ts // CHUNK
    npb = bp * ncp
    nb = n // CHUNK
    rows_per_blk = CHUNK // SUBLANES

    def seq_s(s):
        return jnp.maximum(s - npb, 0) // ncs

    def col(c):
        return pl.BlockSpec((CHUNK, DN_QK_W), lambda s, c=c: (s, c))

    def prev(c):
        return pl.BlockSpec((SUBLANES, DN_QK_W), lambda s, c=c: (jnp.maximum(s * rows_per_blk - 1, 0), c))

    def cprev(c):
        return pl.BlockSpec((None, 1, CONV_W - 1, DN_QK_W), lambda s, c=c: (layer, seq_s(s), 0, c))

    def cw(c):
        return pl.BlockSpec((CONV_W, DN_QK_W), lambda s, c=c: (0, c))

    kern = functools.partial(_dpre_kernel, npb=npb, ncp=ncp, ncs=ncs)
    return pl.pallas_call(
        kern,
        grid=(nb,),
        in_specs=[
            col(0), col(1), col(2), prev(0), prev(1), prev(2), cprev(0), cprev(1), cprev(2), cw(0), cw(1), cw(2),
            pl.BlockSpec((CHUNK, LANES), lambda s: (s, 0)),
            pl.BlockSpec((SUBLANES, LANES), lambda s: (0, 0)),
        ],
        out_specs=[
            pl.BlockSpec((CHUNK, DN_V_W), lambda s: (s, 0)),
            pl.BlockSpec((2 * CHUNK, DN_QK_W), lambda s: (s, 0)),
            pl.BlockSpec((3 * CHUNK, DN_HEADS * CHUNK), lambda s: (s, 0)),
            pl.BlockSpec((SUBLANES, DN_V_W), lambda s: (s, 0)),
        ],
        out_shape=[
            jax.ShapeDtypeStruct((n, DN_V_W), F32),
            jax.ShapeDtypeStruct((2 * n, DN_QK_W), BF16),
            jax.ShapeDtypeStruct((3 * n, DN_HEADS * CHUNK), BF16),
            jax.ShapeDtypeStruct((nb * SUBLANES, DN_V_W), F32),
        ],
        scratch_shapes=[pltpu.VMEM((3, SUBLANES + CHUNK, DN_QK_W), F32), pltpu.VMEM((3, CHUNK, DN_QK_W), F32)],
        compiler_params=_cparams(("parallel",), SMALL_VMEM_LIMIT),
        name="delta_pre",
    )(p_main, p_main, p_main, p_main, p_main, p_main, state_conv, state_conv, state_conv, conv_w, conv_w, conv_w,
      p_ba, ab)


def _drec_kernel(u_ref, wq_ref, lk_ref, eg_ref, z_ref, sp_ref, dng_ref, o_ref, sop_ref, sos_ref, st_ref,
                 *, npb, ncp, ncs):
    s = pl.program_id(0)
    is_p = s < npb
    cidx = jnp.where(is_p, s % ncp, (s - npb) % ncs)
    first = cidx == 0
    last = cidx == jnp.where(is_p, ncp, ncs) - 1

    @pl.when(first)
    def _():
        for h in range(DN_HEADS):
            st_ref[:, h * DN_DV:(h + 1) * DN_DV] = jnp.where(is_p, 0.0, sp_ref[0, h])

    lane = lax.broadcasted_iota(jnp.int32, (1, PAIR_W), 1)
    left = lane < DN_DV
    dng = dng_ref[...]
    for p in range(N_PAIRS):
        psl = slice(p * PAIR_W, (p + 1) * PAIR_W)
        st = st_ref[:, psl]
        st16 = st.astype(BF16)
        zero = jnp.zeros_like(st16)
        sbd = jnp.concatenate([jnp.where(left, st16, zero), jnp.where(left, zero, st16)], axis=0)
        ws = _dot(wq_ref[:, psl], sbd)
        v_new = (u_ref[:, psl] - ws[:CHUNK]).astype(BF16)
        vzero = jnp.zeros_like(v_new)
        vbd = jnp.concatenate([jnp.where(left, v_new, vzero), jnp.where(left, vzero, v_new)], axis=0)
        t = _dot(lk_ref[:, p * 2 * CHUNK:(p + 1) * 2 * CHUNK], vbd)
        o = ws[CHUNK:] + t[:CHUNK]
        st_ref[:, psl] = st * eg_ref[0:1, psl] + t[CHUNK:]
        for hh in (0, 1):
            sl = slice(p * PAIR_W + hh * DN_DV, p * PAIR_W + (hh + 1) * DN_DV)
            oh = o[:, hh * DN_DV:(hh + 1) * DN_DV]
            o_ref[:, sl] = (_rms(oh, dng) * _silu(z_ref[:, sl])).astype(o_ref.dtype)

    @pl.when(last & is_p)
    def _():
        for h in range(DN_HEADS):
            sop_ref[0, h] = st_ref[:, h * DN_DV:(h + 1) * DN_DV]

    @pl.when(last & jnp.logical_not(is_p))
    def _():
        for h in range(DN_HEADS):
            sos_ref[0, h] = st_ref[:, h * DN_DV:(h + 1) * DN_DV]


def _delta_rec(u, wq, lk, eg, p_main, state_delta, layer, dn_g, bp, tp, bs, ts):
    n = u.shape[0]
    ncp, ncs = tp // CHUNK, ts // CHUNK
    npb = bp * ncp
    nb = n // CHUNK
    zblk = COL_Z // DN_V_W

    def seq_s(s):
        return jnp.maximum(s - npb, 0) // ncs

    def seq_p(s):
        return jnp.minimum(s // ncp, bp - 1)

    kern = functools.partial(_drec_kernel, npb=npb, ncp=ncp, ncs=ncs)
    return pl.pallas_call(
        kern,
        grid=(nb,),
        in_specs=[
            pl.BlockSpec((CHUNK, DN_V_W), lambda s: (s, 0)),
            pl.BlockSpec((2 * CHUNK, DN_QK_W), lambda s: (s, 0)),
            pl.BlockSpec((3 * CHUNK, DN_HEADS * CHUNK), lambda s: (s, 0)),
            pl.BlockSpec((SUBLANES, DN_V_W), lambda s: (s, 0)),
            pl.BlockSpec((CHUNK, DN_V_W), lambda s: (s, zblk)),
            pl.BlockSpec((None, 1, DN_HEADS, DN_DK, DN_DV), lambda s: (layer, seq_s(s), 0, 0, 0)),
            pl.BlockSpec((1, DN_DV), lambda s: (0, 0)),
        ],
        out_specs=[
            pl.BlockSpec((CHUNK, DN_V_W), lambda s: (s, 0)),
            pl.BlockSpec((1, DN_HEADS, DN_DK, DN_DV), lambda s: (seq_p(s), 0, 0, 0)),
            pl.BlockSpec((1, DN_HEADS, DN_DK, DN_DV), lambda s: (seq_s(s), 0, 0, 0)),
        ],
        out_shape=[
            jax.ShapeDtypeStruct((n, DN_V_W), BF16),
            jax.ShapeDtypeStruct((bp, DN_HEADS, DN_DK, DN_DV), F32),
            jax.ShapeDtypeStruct((bs, DN_HEADS, DN_DK, DN_DV), F32),
        ],
        scratch_shapes=[pltpu.VMEM((DN_DK, DN_V_W), F32)],
        compiler_params=_cparams(("arbitrary",), SMALL_VMEM_LIMIT),
        name="delta_rec",
    )(u, wq, lk, eg, p_main, state_delta, dn_g)


def _gmlp_kernel(u_ref, v_ref, lg_ref, lb_ref, w_ref, b_ref, gm_ref, vn_ref, *, npb, sample_len):
    s = pl.program_id(0)
    is_s = s >= npb
    gu = _gelu(u_ref[...])
    gv = _gelu(v_ref[...])
    xc = gv - jnp.mean(gv, axis=-1, keepdims=True)
    var = jnp.mean(xc * xc, axis=-1, keepdims=True)
    vn = xc * lax.rsqrt(var + LN_EPS) * lg_ref[...] + lb_ref[...]

    @pl.when(is_s)
    def _():
        vn_ref[...] = vn

    ri = lax.broadcasted_iota(jnp.int32, (GMLP_CHUNK, GMLP_CHUNK), 0)
    ci = lax.broadcasted_iota(jnp.int32, (GMLP_CHUNK, GMLP_CHUNK), 1)
    same_seq = (ri // sample_len) == (ci // sample_len)
    mask = (ri >= ci) & (same_seq | jnp.logical_not(is_s))
    bias = b_ref[0]
    for g in range(GMLP_GROUPS):
        sl = slice(g * GMLP_GROUP_DIM, (g + 1) * GMLP_GROUP_DIM)
        wl = jnp.where(mask, w_ref[0, g], 0.0).astype(BF16)
        mixed = _dot(wl, vn[:, sl].astype(BF16)) + bias[:, g:g + 1]
        gm_ref[:, sl] = (gu[:, sl] * mixed).astype(gm_ref.dtype)


def _gmlp(p_main, ln_g, ln_b, w2, b2, n_prompt, sample_len):
    n = p_main.shape[0]
    nb = n // GMLP_CHUNK
    npb = n_prompt // GMLP_CHUNK
    ublk = COL_U // GMLP_W
    vblk = COL_V // GMLP_W
    kern = functools.partial(_gmlp_kernel, npb=npb, sample_len=sample_len)
    return pl.pallas_call(
        kern,
        grid=(nb,),
        in_specs=[
            pl.BlockSpec((GMLP_CHUNK, GMLP_W), lambda s: (s, ublk)),
            pl.BlockSpec((GMLP_CHUNK, GMLP_W), lambda s: (s, vblk)),
            pl.BlockSpec((1, GMLP_W), lambda s: (0, 0)),
            pl.BlockSpec((1, GMLP_W), lambda s: (0, 0)),
            pl.BlockSpec((1, GMLP_GROUPS, GMLP_CHUNK, GMLP_CHUNK), lambda s: (jnp.where(s >= npb, 1, 0), 0, 0, 0)),
            pl.BlockSpec((1, GMLP_CHUNK, LANES), lambda s: (jnp.where(s >= npb, 1, 0), 0, 0)),
        ],
        out_specs=[
            pl.BlockSpec((GMLP_CHUNK, GMLP_W), lambda s: (s, 0)),
            pl.BlockSpec((GMLP_CHUNK, GMLP_W), lambda s: (jnp.maximum(s - npb, 0), 0)),
        ],
        out_shape=[
            jax.ShapeDtypeStruct((n, GMLP_W), BF16),
            jax.ShapeDtypeStruct((n - n_prompt, GMLP_W), F32),
        ],
        compiler_params=_cparams(("arbitrary",), SMALL_VMEM_LIMIT),
        name="gmlp",
    )(p_main, p_main, ln_g, ln_b, w2, b2)


def _merge_kernel(o_ref, gm_ref, wd_ref, wg_ref, ga_ref, gb_ref, y_ref):
    a = _dot(o_ref[...], wd_ref[...])
    b = _dot(gm_ref[...], wg_ref[...])
    y_ref[...] = (jax.nn.sigmoid(ga_ref[...]) * a + jax.nn.sigmoid(gb_ref[...]) * b).astype(y_ref.dtype)


def _merge(o, gm, w_dn, w_gm, p_main):
    n = o.shape[0]
    tm = _tile(n, 512)
    tn = 1024
    ga0, gb0 = COL_GA // tn, COL_GB // tn
    return pl.pallas_call(
        _merge_kernel,
        grid=(D_MODEL // tn, n // tm),
        in_specs=[
            pl.BlockSpec((tm, DN_V_W), lambda j, i: (i, 0)),
            pl.BlockSpec((tm, GMLP_W), lambda j, i: (i, 0)),
            pl.BlockSpec((DN_V_W, tn), lambda j, i: (0, j)),
            pl.BlockSpec((GMLP_W, tn), lambda j, i: (0, j)),
            pl.BlockSpec((tm, tn), lambda j, i: (i, ga0 + j)),
            pl.BlockSpec((tm, tn), lambda j, i: (i, gb0 + j)),
        ],
        out_specs=pl.BlockSpec((tm, tn), lambda j, i: (i, j)),
        out_shape=jax.ShapeDtypeStruct((n, D_MODEL), BF16),
        compiler_params=_cparams(("parallel", "parallel")),
        name="merge",
    )(o, gm, w_dn, w_gm, p_main, p_main)


def _outproj_kernel(y_ref, w_ref, x_ref, o_ref):
    o_ref[...] = x_ref[...] + _dot(y_ref[...], w_ref[...])


def _outproj(y, w, x):
    n = y.shape[0]
    tm = _tile(n, 512)
    tn = 1024
    return pl.pallas_call(
        _outproj_kernel,
        grid=(D_MODEL // tn, n // tm),
        in_specs=[
            pl.BlockSpec((tm, D_MODEL), lambda j, i: (i, 0)),
            pl.BlockSpec((D_MODEL, tn), lambda j, i: (0, j)),
            pl.BlockSpec((tm, tn), lambda j, i: (i, j)),
        ],
        out_specs=pl.BlockSpec((tm, tn), lambda j, i: (i, j)),
        out_shape=jax.ShapeDtypeStruct((n, D_MODEL), F32),
        compiler_params=_cparams(("parallel", "parallel")),
        name="out_proj",
    )(y, w, x)


def _ffn_kernel(x_ref, g_ref, wg_ref, wu_ref, wd_ref, o_ref, hn_ref):
    @pl.when(pl.program_id(1) == 0)
    def _():
        x = x_ref[...]
        hn_ref[...] = _rms(x, g_ref[...]).astype(BF16)
        o_ref[...] = x

    hn = hn_ref[...]
    hid = (_silu(_dot(hn, wg_ref[...])) * _dot(hn, wu_ref[...])).astype(BF16)
    o_ref[...] += _dot(hid, wd_ref[...])


def _ffn(x, g, wg, wu, wd):
    n = x.shape[0]
    tm = _tile(n, 1024)
    tf = 512
    return pl.pallas_call(
        _ffn_kernel,
        grid=(n // tm, D_FF // tf),
        in_specs=[
            pl.BlockSpec((tm, D_MODEL), lambda i, j: (i, 0)),
            pl.BlockSpec((1, D_MODEL), lambda i, j: (0, 0)),
            pl.BlockSpec((D_MODEL, tf), lambda i, j: (0, j)),
            pl.BlockSpec((D_MODEL, tf), lambda i, j: (0, j)),
            pl.BlockSpec((tf, D_MODEL), lambda i, j: (j, 0)),
        ],
        out_specs=pl.BlockSpec((tm, D_MODEL), lambda i, j: (i, 0)),
        out_shape=jax.ShapeDtypeStruct((n, D_MODEL), F32),
        scratch_shapes=[pltpu.VMEM((tm, D_MODEL), BF16)],
        compiler_params=_cparams(("parallel", "arbitrary")),
        name="ffn_dense",
    )(x, g, wg, wu, wd)


def _router_kernel(x_ref, g_ref, rw_ref, rb_ref, meta_ref, cnt_ref, carry_ref):
    @pl.when(pl.program_id(0) == 0)
    def _():
        carry_ref[...] = jnp.zeros_like(carry_ref)

    tm = x_ref.shape[0]
    hn = _rms(x_ref[...], g_ref[...])
    logits = jnp.dot(hn, rw_ref[...], precision=HIGHEST, preferred_element_type=F32) + rb_ref[...]
    lane = lax.broadcasted_iota(jnp.int32, (tm, LANES), 1).astype(F32)
    m1 = jnp.max(logits, axis=-1, keepdims=True)
    i1 = jnp.min(jnp.where(logits == m1, lane, float(LANES)), axis=-1, keepdims=True)
    oh1 = lane == i1
    rest = jnp.where(oh1, -jnp.inf, logits)
    m2 = jnp.max(rest, axis=-1, keepdims=True)
    i2 = jnp.min(jnp.where(rest == m2, lane, float(LANES)), axis=-1, keepdims=True)
    oh2 = lane == i2
    e = jnp.exp(m2 - m1)
    p1 = 1.0 / (1.0 + e)
    p2 = e / (1.0 + e)

    onehot = jnp.where(oh1 | oh2, 1.0, 0.0)
    ri = lax.broadcasted_iota(jnp.int32, (tm, tm), 0)
    ci = lax.broadcasted_iota(jnp.int32, (tm, tm), 1)
    before = jnp.where(ri > ci, 1.0, 0.0).astype(BF16)
    rank = _dot(before, onehot.astype(BF16)) + carry_ref[...]
    r1 = jnp.sum(jnp.where(oh1, rank, 0.0), axis=-1, keepdims=True)
    r2 = jnp.sum(jnp.where(oh2, rank, 0.0), axis=-1, keepdims=True)
    carry_ref[...] += jnp.sum(onehot, axis=0, keepdims=True)

    meta = jnp.zeros((tm, LANES), F32)
    for idx, val in enumerate((i1, i2, p1, p2, r1, r2)):
        meta = jnp.where(lane == float(idx), val, meta)
    meta_ref[...] = meta
    cnt_ref[...] = carry_ref[...]


def _router(x, g, rw, rb):
    n = x.shape[0]
    tm = _tile(n, 256)
    return pl.pallas_call(
        _router_kernel,
        grid=(n // tm,),
        in_specs=[
            pl.BlockSpec((tm, D_MODEL), lambda i: (i, 0)),
            pl.BlockSpec((1, D_MODEL), lambda i: (0, 0)),
            pl.BlockSpec((D_MODEL, LANES), lambda i: (0, 0)),
            pl.BlockSpec((1, LANES), lambda i: (0, 0)),
        ],
        out_specs=[
            pl.BlockSpec((tm, LANES), lambda i: (i, 0)),
            pl.BlockSpec((1, LANES), lambda i: (0, 0)),
        ],
        out_shape=[jax.ShapeDtypeStruct((n, LANES), F32), jax.ShapeDtypeStruct((1, LANES), F32)],
        scratch_shapes=[pltpu.VMEM((1, LANES), F32)],
        compiler_params=_cparams(("arbitrary",), SMALL_VMEM_LIMIT),
        name="moe_router",
    )(x, g, rw, rb)


DMA_UNROLL = 8


def _gather_tile(idx_ref, idx_base, rows, src_ref, dst_ref, dst_base, sem, *, wait):
    def body(r, carry):
        cp = pltpu.make_async_copy(src_ref.at[idx_ref[idx_base + r]], dst_ref.at[dst_base + r], sem)
        if wait:
            cp.wait()
        else:
            cp.start()
        return carry
    lax.fori_loop(0, rows, body, 0, unroll=DMA_UNROLL)


def _cm_stride(rows):
    return rows + SUBLANES


def _rows_to_chunk_major(rows_ref, cm_ref, rows, stride):
    def body(r, carry):
        cm_ref[pl.ds(r, ROW_CHUNKS, stride=stride), :] = rows_ref[r]
        return carry
    lax.fori_loop(0, rows, body, 0, unroll=DMA_UNROLL)


def _chunk_major_to_rows(cm_ref, rows_ref, rows, stride):
    def body(r, carry):
        rows_ref[r] = cm_ref[pl.ds(r, ROW_CHUNKS, stride=stride), :]
        return carry
    lax.fori_loop(0, rows, body, 0, unroll=DMA_UNROLL)


def _expert_kernel(src_ref, te_ref, na_ref, x3_ref, g_ref, wg_ref, wu_ref, wd_ref, o_ref, buf_ref, hn_ref, cm_ref,
                   sem):
    i = pl.program_id(0)
    j = pl.program_id(1)
    nj = pl.num_programs(1)
    rows = hn_ref.shape[0]
    stride = cm_ref.shape[0] // ROW_CHUNKS
    n_active = na_ref[0]
    active = i < n_active
    slot = i % 2

    def chunk(c):
        return pl.ds(c * stride, rows)

    @pl.when((j == 0) & active)
    def _():
        @pl.when(i == 0)
        def _():
            _gather_tile(src_ref, 0, rows, x3_ref, buf_ref.at[0], 0, sem.at[0], wait=False)

        @pl.when(i + 1 < n_active)
        def _():
            _gather_tile(src_ref, (i + 1) * rows, rows, x3_ref, buf_ref.at[1 - slot], 0, sem.at[1 - slot], wait=False)

        _gather_tile(src_ref, i * rows, rows, x3_ref, buf_ref.at[slot], 0, sem.at[slot], wait=True)
        _rows_to_chunk_major(buf_ref.at[slot], cm_ref, rows, stride)

        ss = jnp.zeros((rows, 1), F32)
        for c in range(ROW_CHUNKS):
            xc = cm_ref[chunk(c), :]
            ss = ss + jnp.sum(xc * xc, axis=-1, keepdims=True)
        scale = lax.rsqrt(ss / D_MODEL + RMS_EPS)
        for c in range(ROW_CHUNKS):
            csl = slice(c * LANES, (c + 1) * LANES)
            hn_ref[:, csl] = (cm_ref[chunk(c), :] * scale * g_ref[:, csl]).astype(BF16)

    @pl.when(j == 0)
    def _():
        cm_ref[...] = jnp.zeros_like(cm_ref)

    @pl.when(active)
    def _():
        hn = hn_ref[...]
        hid = (_silu(_dot(hn, wg_ref[0])) * _dot(hn, wu_ref[0])).astype(BF16)
        out = _dot(hid, wd_ref[0])
        for c in range(ROW_CHUNKS):
            cm_ref[chunk(c), :] += out[:, c * LANES:(c + 1) * LANES]

    @pl.when(j == nj - 1)
    def _():
        _chunk_major_to_rows(cm_ref, o_ref, rows, stride)


def _experts(x3, src, g, wg, wu, wd, tile_expert, n_active, te_rows):
    s_max = src.shape[0]
    n_tiles = s_max // te_rows
    tf = 256
    nj = D_FF_EXPERT // tf

    def row(i, na):
        return jnp.minimum(i, na[0] - 1)

    def jj(i, j, na):
        return jnp.where(i < na[0], j, nj - 1)

    return pl.pallas_call(
        _expert_kernel,
        grid_spec=pltpu.PrefetchScalarGridSpec(
            num_scalar_prefetch=3,
            grid=(n_tiles, nj),
            in_specs=[
                pl.BlockSpec(memory_space=pl.ANY),
                pl.BlockSpec((1, D_MODEL), lambda i, j, sr, te, na: (0, 0)),
                pl.BlockSpec((1, D_MODEL, tf), lambda i, j, sr, te, na: (te[row(i, na)], 0, jj(i, j, na))),
                pl.BlockSpec((1, D_MODEL, tf), lambda i, j, sr, te, na: (te[row(i, na)], 0, jj(i, j, na))),
                pl.BlockSpec((1, tf, D_MODEL), lambda i, j, sr, te, na: (te[row(i, na)], jj(i, j, na), 0)),
            ],
            out_specs=pl.BlockSpec((te_rows, ROW_CHUNKS, LANES), lambda i, j, sr, te, na: (i, 0, 0)),
            scratch_shapes=[
                pltpu.VMEM((2, te_rows, ROW_CHUNKS, LANES), F32),
                pltpu.VMEM((te_rows, D_MODEL), BF16),
                pltpu.VMEM((ROW_CHUNKS * _cm_stride(te_rows), LANES), F32),
                pltpu.SemaphoreType.DMA((2,)),
            ],
        ),
        out_shape=jax.ShapeDtypeStruct((s_max, ROW_CHUNKS, LANES), F32),
        compiler_params=_cparams(("arbitrary", "arbitrary")),
        name="moe_experts",
    )(src, tile_expert, n_active, x3, g, wg, wu, wd)


def _combine_kernel(back_ref, ys3_ref, x_ref, meta_ref, g_ref, y_ref, buf_ref, cm_ref, sem, *, tile0, n_tok):
    i = pl.program_id(0)
    nt = pl.num_programs(0)
    tm = x_ref.shape[0]
    stride = cm_ref.shape[0] // ROW_CHUNKS
    slot = i % 2

    def gather(tile, slt, wait):
        for kk in range(TOP_K):
            _gather_tile(back_ref, kk * n_tok + (tile0 + tile) * tm, tm, ys3_ref, buf_ref.at[slt], kk * tm,
                         sem.at[slt], wait=wait)

    @pl.when(i == 0)
    def _():
        gather(0, 0, False)

    @pl.when(i + 1 < nt)
    def _():
        gather(i + 1, 1 - slot, False)

    gather(i, slot, True)
    _rows_to_chunk_major(buf_ref.at[slot], cm_ref, TOP_K * tm, stride)

    meta = meta_ref[...]
    p1 = meta[:, 2:3]
    p2 = meta[:, 3:4]
    ss = jnp.zeros((tm, 1), F32)
    for c in range(ROW_CHUNKS):
        csl = slice(c * LANES, (c + 1) * LANES)
        e1 = cm_ref[pl.ds(c * stride, tm), :]
        e2 = cm_ref[pl.ds(c * stride + tm, tm), :]
        xn = x_ref[:, csl] + (p1 * e1 + p2 * e2)
        ss = ss + jnp.sum(xn * xn, axis=-1, keepdims=True)
        y_ref[:, csl] = xn
    y_ref[...] = y_ref[...] * lax.rsqrt(ss / D_MODEL + RMS_EPS) * g_ref[...]


def _combine(x, ys3, back, meta, g, row0, rows):
    n = x.shape[0]
    tm = _tile(math.gcd(math.gcd(row0, rows), n), 256)
    b0 = row0 // tm
    kern = functools.partial(_combine_kernel, tile0=b0, n_tok=n)
    return pl.pallas_call(
        kern,
        grid_spec=pltpu.PrefetchScalarGridSpec(
            num_scalar_prefetch=1,
            grid=(rows // tm,),
            in_specs=[
                pl.BlockSpec(memory_space=pl.ANY),
                pl.BlockSpec((tm, D_MODEL), lambda i, bk: (b0 + i, 0)),
                pl.BlockSpec((tm, LANES), lambda i, bk: (b0 + i, 0)),
                pl.BlockSpec((1, D_MODEL), lambda i, bk: (0, 0)),
            ],
            out_specs=pl.BlockSpec((tm, D_MODEL), lambda i, bk: (i, 0)),
            scratch_shapes=[
                pltpu.VMEM((2, TOP_K * tm, ROW_CHUNKS, LANES), F32),
                pltpu.VMEM((ROW_CHUNKS * _cm_stride(TOP_K * tm), LANES), F32),
                pltpu.SemaphoreType.DMA((2,)),
            ],
        ),
        out_shape=jax.ShapeDtypeStruct((rows, D_MODEL), F32),
        compiler_params=_cparams(("arbitrary",), SMALL_VMEM_LIMIT),
        name="moe_combine",
    )(back, ys3, x, meta, g)


EXPERT_TILE = 512


def _routing_tables(meta, counts, n):
    te = EXPERT_TILE
    i1 = meta[:, 0].astype(jnp.int32)
    i2 = meta[:, 1].astype(jnp.int32)
    r1 = meta[:, 4].astype(jnp.int32)
    r2 = meta[:, 5].astype(jnp.int32)
    cnt = counts[0, :N_EXPERTS].astype(jnp.int32)
    padded = (cnt + te - 1) // te * te
    ends = jnp.cumsum(padded)
    offs = ends - padded
    slot1 = offs[i1] + r1
    slot2 = offs[i2] + r2
    s_max = (TOP_K * n + N_EXPERTS * (te - 1) + te - 1) // te * te
    tok = jnp.arange(n, dtype=jnp.int32)
    src = jnp.zeros((s_max,), jnp.int32).at[slot1].set(tok).at[slot2].set(tok)
    n_active = (ends[-1] // te).astype(jnp.int32).reshape(1)
    starts = jnp.arange(s_max // te, dtype=jnp.int32) * te
    tile_expert = jnp.minimum(jnp.searchsorted(ends, starts, side="right"), N_EXPERTS - 1).astype(jnp.int32)
    return src, jnp.concatenate([slot1, slot2]), tile_expert, n_active


def _prep_in_weights(w_in):
    w_main = jnp.concatenate([w_in[:, :OFF_B], w_in[:, OFF_U:]], axis=1).astype(BF16)
    w_ba = jnp.pad(w_in[:, OFF_B:OFF_U], ((0, 0), (0, LANES - 2 * DN_HEADS))).astype(BF16)
    return w_main, w_ba


def _prep_decay_params(a_log, dt_bias):
    rows = jnp.zeros((SUBLANES, LANES), F32)
    rows = rows.at[0, DN_HEADS:2 * DN_HEADS].set(a_log)
    rows = rows.at[1, DN_HEADS:2 * DN_HEADS].set(dt_bias)
    return rows


def _prep_gmlp_params(sp_w, sp_b, sample_len):
    reps = GMLP_CHUNK // sample_len
    w_s = jnp.tile(sp_w[:, :sample_len, :sample_len], (1, reps, reps))
    b_s = jnp.tile(sp_b[:, :sample_len], (1, reps))
    w2 = jnp.stack([sp_w, w_s])
    b2 = jnp.stack([sp_b.T, b_s.T])
    b2 = jnp.pad(b2, ((0, 0), (0, 0), (0, LANES - GMLP_GROUPS)))
    return w2, b2


def _last_rows(p_main, row0, b, t):
    rows = (row0 + (jnp.arange(b, dtype=jnp.int32) * t + t - (CONV_W - 1))[:, None]
            + jnp.arange(CONV_W - 1, dtype=jnp.int32)[None, :])
    return jnp.take(p_main, rows.reshape(-1), axis=0)[:, :DN_QKV_W].reshape(b, CONV_W - 1, DN_QKV_W)


def kernel(x_prompt, x_sample, state_conv, state_delta, norm_mix_g, w_in, conv_w, a_log, dt_bias, dn_norm_g,
           gm_ln_g, gm_ln_b, sp_w, sp_b, w_dn_out, w_gm_out, w_out, norm_ffn_g, ffn_wg, ffn_wu, ffn_wd,
           router_w, router_b, moe_wg, moe_wu, moe_wd, final_g):
    bp, tp, _ = x_prompt.shape
    bs, ts, _ = x_sample.shape
    n_p, n_s = bp * tp, bs * ts
    n = n_p + n_s
    assert tp % GMLP_CHUNK == 0 and GMLP_CHUNK % ts == 0 and n_s % GMLP_CHUNK == 0 and ts % CHUNK == 0

    x = jnp.concatenate([x_prompt.reshape(n_p, D_MODEL), x_sample.reshape(n_s, D_MODEL)], axis=0)
    conv_p, conv_s, delta_p, delta_s, v_s = [], [], [], [], []
    y_p = y_s = None
    for l in range(DEPTH):
        w_main, w_ba = _prep_in_weights(w_in[l])
        p_main, p_ba = _inproj(x, norm_mix_g[l].reshape(1, D_MODEL), w_main, w_ba)
        u, wq, lk, eg = _delta_pre(p_main, p_ba, state_conv, l, conv_w[l], _prep_decay_params(a_log[l], dt_bias[l]),
                                   bp, tp, bs, ts)
        o, sp_out, ss_out = _delta_rec(u, wq, lk, eg, p_main, state_delta, l, dn_norm_g[l].reshape(1, DN_DV),
                                       bp, tp, bs, ts)
        w2, b2 = _prep_gmlp_params(sp_w[l], sp_b[l], ts)
        gm, vn_s = _gmlp(p_main, gm_ln_g[l].reshape(1, GMLP_W), gm_ln_b[l].reshape(1, GMLP_W), w2, b2, n_p, ts)
        y = _merge(o, gm, w_dn_out[l].astype(BF16), w_gm_out[l].astype(BF16), p_main)
        x = _outproj(y, w_out[l].astype(BF16), x)

        conv_p.append(_last_rows(p_main, 0, bp, tp))
        conv_s.append(_last_rows(p_main, n_p, bs, ts))
        delta_p.append(sp_out)
        delta_s.append(ss_out)
        v_s.append(vn_s.reshape(bs, ts, GMLP_W))

        g_ffn = norm_ffn_g[l].reshape(1, D_MODEL)
        if l % 2 == 0:
            x = _ffn(x, g_ffn, ffn_wg[l // 2].astype(BF16), ffn_wu[l // 2].astype(BF16), ffn_wd[l // 2].astype(BF16))
        else:
            e = l // 2
            rw = jnp.pad(router_w[e], ((0, 0), (0, LANES - N_EXPERTS)))
            rb = jnp.pad(router_b[e], (0, LANES - N_EXPERTS), constant_values=-jnp.inf).reshape(1, LANES)
            meta, counts = _router(x, g_ffn, rw, rb)
            src, back, tile_expert, n_active = _routing_tables(meta, counts, n)
            ys3 = _experts(x.reshape(n, ROW_CHUNKS, LANES), src, g_ffn, moe_wg[e].astype(BF16),
                           moe_wu[e].astype(BF16), moe_wd[e].astype(BF16), tile_expert, n_active, EXPERT_TILE)
            if l == DEPTH - 1:
                fg = final_g.reshape(1, D_MODEL)
                y_p = _combine(x, ys3, back, meta, fg, 0, n_p)
                y_s = _combine(x, ys3, back, meta, fg, n_p, n_s)
    return (y_p.reshape(bp, tp, D_MODEL), y_s.reshape(bs, ts, D_MODEL), jnp.stack(conv_p), jnp.stack(delta_p),
            jnp.stack(conv_s), jnp.stack(delta_s), jnp.stack(v_s))
```

```python
import functools
import math

import jax
import jax.numpy as jnp
from jax import lax
from jax.experimental import pallas as pl
from jax.experimental.pallas import tpu as pltpu

F32 = jnp.float32
BF16 = jnp.bfloat16
HIGHEST = lax.Precision.HIGHEST

D_MODEL = 2048
DEPTH = 2
CHUNK = 64
DN_HEADS = 16
DN_DK = 128
DN_DV = 128
DN_QK_W = DN_HEADS * DN_DK
DN_V_W = DN_HEADS * DN_DV
DN_QKV_W = 2 * DN_QK_W + DN_V_W
CONV_W = 4
DN_SCALE = DN_DK ** -0.5
GMLP_CHUNK = 128
GMLP_GROUPS = 16
GMLP_GROUP_DIM = 128
GMLP_W = GMLP_GROUPS * GMLP_GROUP_DIM
OFF_Z = DN_QKV_W
OFF_B = OFF_Z + DN_V_W
OFF_A = OFF_B + DN_HEADS
OFF_U = OFF_A + DN_HEADS
OFF_V = OFF_U + GMLP_W
OFF_GA = OFF_V + GMLP_W
OFF_GB = OFF_GA + D_MODEL
IN_W = OFF_GB + D_MODEL
D_FF = 11 * D_MODEL // 4
N_EXPERTS = 8
TOP_K = 2
D_FF_EXPERT = D_FF // 2
RMS_EPS = 1e-6
LN_EPS = 1e-5
L2_EPS = 1e-6

LANES = 128
SUBLANES = 8
V7X_VMEM_LIMIT = 56 * 1024 * 1024
SMALL_VMEM_LIMIT = 32 * 1024 * 1024

MAIN_W = IN_W - 2 * DN_HEADS
COL_Z = OFF_Z
COL_U = COL_Z + DN_V_W
COL_V = COL_U + GMLP_W
COL_GA = COL_V + GMLP_W
COL_GB = COL_GA + D_MODEL

ROW_CHUNKS = D_MODEL // LANES
N_PAIRS = DN_HEADS // 2
PAIR_W = 2 * DN_DK
PREV_ROWS = 16
PAIR_GROUP = 8


def _tile(n, pref):
    t = pref
    while n % t:
        t //= 2
    return t


def _cparams(sem, vmem_bytes=V7X_VMEM_LIMIT):
    return pltpu.CompilerParams(dimension_semantics=sem, vmem_limit_bytes=vmem_bytes)


def _rms(x, g):
    ms = jnp.mean(x * x, axis=-1, keepdims=True)
    return x * lax.rsqrt(ms + RMS_EPS) * g


def _dot(a, b):
    return jnp.dot(a, b, preferred_element_type=F32)


def _dot_nt(a, b):
    return lax.dot_general(a, b, (((1,), (1,)), ((), ())), preferred_element_type=F32)


def _silu(x):
    return x * jax.nn.sigmoid(x)


def _gelu(x):
    return 0.5 * x * (1.0 + jnp.tanh(math.sqrt(2.0 / math.pi) * (x + 0.044715 * (x * x * x))))


def _softplus(x):
    return jnp.maximum(x, 0.0) + jnp.log1p(jnp.exp(-jnp.abs(x)))


BA_W = 2 * DN_HEADS


def _row_sources(parts, block_shape, index_fn):
    if len(parts) == 1:
        return [pl.BlockSpec(block_shape, index_fn)], 0
    tm = block_shape[0]
    assert len(parts) == 2 and all(p.shape[0] % tm == 0 for p in parts)
    t0 = parts[0].shape[0] // tm

    def first(*ids):
        r, c = index_fn(*ids)
        return jnp.minimum(r, t0 - 1), c

    def second(*ids):
        r, c = index_fn(*ids)
        return jnp.maximum(r - t0, 0), c

    return [pl.BlockSpec(block_shape, first), pl.BlockSpec(block_shape, second)], t0


def _pick_rows(refs, first_tiles, row_axis):
    if len(refs) == 1:
        return refs[0][...]
    return jnp.where(pl.program_id(row_axis) < first_tiles, refs[0][...], refs[1][...])


def _mixnorm_kernel(*refs, first_tiles):
    *x_refs, g_ref, wbat_ref, hn_ref, pba_ref = refs
    hn = _rms(_pick_rows(x_refs, first_tiles, 0), g_ref[...]).astype(BF16)
    hn_ref[...] = hn
    pba_ref[...] = _dot_nt(hn, wbat_ref[...].astype(BF16))


def _mixnorm(x_parts, g, w_ba_t):
    n = sum(p.shape[0] for p in x_parts)
    tm = _tile(math.gcd(*[p.shape[0] for p in x_parts]), 512)
    x_specs, first_tiles = _row_sources(x_parts, (tm, D_MODEL), lambda i: (i, 0))
    return pl.pallas_call(
        functools.partial(_mixnorm_kernel, first_tiles=first_tiles),
        grid=(n // tm,),
        in_specs=x_specs + [
            pl.BlockSpec((1, D_MODEL), lambda i: (0, 0)),
            pl.BlockSpec((LANES, D_MODEL), lambda i: (0, 0)),
        ],
        out_specs=[
            pl.BlockSpec((tm, D_MODEL), lambda i: (i, 0)),
            pl.BlockSpec((tm, LANES), lambda i: (i, 0)),
        ],
        out_shape=[jax.ShapeDtypeStruct((n, D_MODEL), BF16), jax.ShapeDtypeStruct((n, LANES), F32)],
        compiler_params=_cparams(("parallel",), SMALL_VMEM_LIMIT),
        name="mix_norm",
    )(*x_parts, g, w_ba_t)


def _inproj_kernel(hn_ref, wa_ref, wb_ref, p_ref, w16_ref, *, first_shifted):
    j = pl.program_id(0)

    @pl.when(pl.program_id(1) == 0)
    def _():
        @pl.when(j < first_shifted)
        def _():
            w16_ref[...] = wa_ref[...].T.astype(BF16)

        @pl.when(j >= first_shifted)
        def _():
            wt = jnp.concatenate([wa_ref[BA_W:, :], wb_ref[:BA_W, :]], axis=0)
            w16_ref[...] = wt.T.astype(BF16)

    p_ref[...] = _dot(hn_ref[...], w16_ref[...]).astype(p_ref.dtype)


def _inproj(hn, w_in_t, layer):
    n = hn.shape[0]
    tm = _tile(n, 1024)
    tn = 1024
    assert OFF_B % tn == 0 and OFF_U - OFF_B == BA_W and BA_W % SUBLANES == 0
    first_shifted = OFF_B // tn
    tail_blocks = tn // LANES
    kern = functools.partial(_inproj_kernel, first_shifted=first_shifted)
    return pl.pallas_call(
        kern,
        grid=(MAIN_W // tn, n // tm),
        in_specs=[
            pl.BlockSpec((tm, D_MODEL), lambda j, i: (i, 0)),
            pl.BlockSpec((None, tn, D_MODEL), lambda j, i: (layer, j, 0)),
            pl.BlockSpec((None, LANES, D_MODEL), lambda j, i: (layer, (j + 1) * tail_blocks, 0)),
        ],
        out_specs=pl.BlockSpec((tm, tn), lambda j, i: (i, j)),
        out_shape=jax.ShapeDtypeStruct((n, MAIN_W), BF16),
        scratch_shapes=[pltpu.VMEM((D_MODEL, tn), BF16)],
        compiler_params=_cparams(("parallel", "arbitrary")),
        name="in_proj",
    )(hn, w_in_t, w_in_t)


def _dpre_kernel(q_ref, k_ref, v_ref, pq_ref, pk_ref, pv_ref, cq_ref, ck_ref, cv_ref, cwq_ref, cwk_ref, cwv_ref,
                 ba_ref, ab_ref, u_ref, wqo_ref, lk_ref, eg_ref, xbuf_ref, cbuf_ref, *, npb, ncp, ncs):
    s = pl.program_id(0)
    is_p = s < npb
    first = jnp.where(is_p, s % ncp, (s - npb) % ncs) == 0
    hist = SUBLANES - (CONV_W - 1)

    def conv_silu(j, raw_ref, prev_ref, cp_ref, cw_ref):
        raw = raw_ref[...].astype(F32)
        xbuf_ref[j, 0:SUBLANES, :] = prev_ref[prev_ref.shape[0] - SUBLANES:, :].astype(F32)

        @pl.when(first)
        def _():
            xbuf_ref[j, hist:SUBLANES, :] = jnp.where(is_p, 0.0, cp_ref[0])

        xbuf_ref[j, SUBLANES:SUBLANES + CHUNK, :] = raw
        acc = raw * cw_ref[CONV_W - 1:CONV_W, :]
        for sft in range(1, CONV_W):
            acc = acc + xbuf_ref[j, SUBLANES - sft:SUBLANES - sft + CHUNK, :] * cw_ref[CONV_W - 1 - sft:CONV_W - sft, :]
        return _silu(acc)

    cbuf_ref[0] = conv_silu(0, q_ref, pq_ref, cq_ref, cwq_ref)
    cbuf_ref[1] = conv_silu(1, k_ref, pk_ref, ck_ref, cwk_ref)
    cbuf_ref[2] = conv_silu(2, v_ref, pv_ref, cv_ref, cwv_ref)

    ba = ba_ref[...]
    lane = lax.broadcasted_iota(jnp.int32, (CHUNK, LANES), 1)
    gval = -jnp.exp(ab_ref[0:1, :]) * _softplus(ba + ab_ref[1:2, :])
    bg = jnp.where(lane < DN_HEADS, jax.nn.sigmoid(ba), gval)
    r64 = lax.broadcasted_iota(jnp.int32, (CHUNK, CHUNK), 0)
    c64 = lax.broadcasted_iota(jnp.int32, (CHUNK, CHUNK), 1)
    gc_cols = jnp.dot((r64 >= c64).astype(F32), bg, precision=HIGHEST, preferred_element_type=F32)
    gc_rows = gc_cols.T
    gc_rows2 = jnp.concatenate([gc_rows, gc_rows], axis=1)

    ri = lax.broadcasted_iota(jnp.int32, (2 * CHUNK, 2 * CHUNK), 0)
    ci = lax.broadcasted_iota(jnp.int32, (2 * CHUNK, 2 * CHUNK), 1)
    same = (ri // CHUNK) == (ci // CHUNK)
    causal = same & (ri >= ci)
    strict = same & (ri > ci)
    left = lax.broadcasted_iota(jnp.int32, (1, 2 * CHUNK), 1) < CHUNK

    def stack(j, p):
        return jnp.concatenate([cbuf_ref[j, :, (2 * p) * DN_DK:(2 * p + 1) * DN_DK],
                                cbuf_ref[j, :, (2 * p + 1) * DN_DK:(2 * p + 2) * DN_DK]], axis=0)

    def colpair(a, off, p):
        return jnp.concatenate([a[:, off + 2 * p:off + 2 * p + 1], a[:, off + 2 * p + 1:off + 2 * p + 2]], axis=0)

    def pair_group(pairs):
        beta = {p: colpair(bg, 0, p) for p in pairs}
        gcc = {p: colpair(gc_cols, DN_HEADS, p) for p in pairs}
        gcr = {p: jnp.where(left, gc_rows2[DN_HEADS + 2 * p:DN_HEADS + 2 * p + 1, :],
                            gc_rows2[DN_HEADS + 2 * p + 1:DN_HEADS + 2 * p + 2, :]) for p in pairs}
        gl = {p: [gc_cols[CHUNK - 1:CHUNK, DN_HEADS + 2 * p + t:DN_HEADS + 2 * p + t + 1] for t in (0, 1)]
              for p in pairs}
        glr = {p: jnp.where(left, gl[p][0], gl[p][1]) for p in pairs}
        decay = {p: jnp.exp(jnp.where(causal, gcc[p] - gcr[p], -jnp.inf)) for p in pairs}
        egc = {p: jnp.exp(gcc[p]) for p in pairs}

        qh = {p: stack(0, p) for p in pairs}
        kh = {p: stack(1, p) for p in pairs}
        q = {p: qh[p] * lax.rsqrt(jnp.sum(qh[p] * qh[p], axis=-1, keepdims=True) + L2_EPS) * DN_SCALE for p in pairs}
        k = {p: kh[p] * lax.rsqrt(jnp.sum(kh[p] * kh[p], axis=-1, keepdims=True) + L2_EPS) for p in pairs}
        kb = {p: k[p] * beta[p] for p in pairs}
        k16 = {p: k[p].astype(BF16) for p in pairs}

        m = {p: jnp.where(strict, -(_dot_nt(kb[p].astype(BF16), k16[p]) * decay[p]), 0.0) for p in pairs}
        r = dict(m)
        for _ in range(5):
            m16 = {p: m[p].astype(BF16) for p in pairs}
            m = {p: _dot(m16[p], m16[p]) for p in pairs}
            r = {p: r[p] + m[p] + _dot(r[p].astype(BF16), m[p].astype(BF16)) for p in pairs}
        rhs = {p: jnp.concatenate([stack(2, p) * beta[p], kb[p] * egc[p]], axis=1) for p in pairs}
        uw = {p: rhs[p] + _dot(r[p].astype(BF16), rhs[p].astype(BF16)) for p in pairs}
        qk = {p: _dot_nt(q[p].astype(BF16), k16[p]) * decay[p] for p in pairs}
        kdt = {p: k[p].T * jnp.exp(glr[p] - gcr[p]) for p in pairs}

        for p in pairs:
            for t in (0, 1):
                sl = slice((2 * p + t) * DN_DK, (2 * p + t + 1) * DN_DK)
                rows = slice(t * CHUNK, (t + 1) * CHUNK)
                u_ref[:, sl] = uw[p][rows, :DN_DV]
                wqo_ref[0:CHUNK, sl] = uw[p][rows, DN_DV:].astype(BF16)
                wqo_ref[CHUNK:2 * CHUNK, sl] = (q[p][rows] * egc[p][rows]).astype(BF16)
                eg_ref[:, sl] = jnp.broadcast_to(jnp.exp(gl[p][t]), (SUBLANES, DN_DV))
            psl = slice(p * 2 * CHUNK, (p + 1) * 2 * CHUNK)
            lk_ref[0:CHUNK, psl] = jnp.where(left, qk[p][:CHUNK], qk[p][CHUNK:]).astype(BF16)
            lk_ref[CHUNK:3 * CHUNK, psl] = kdt[p].astype(BF16)

    for first_pair in range(0, N_PAIRS, PAIR_GROUP):
        pair_group(range(first_pair, first_pair + PAIR_GROUP))


def _delta_pre(p_main, p_ba, state_conv, layer, conv_w, ab, bp, tp, bs, ts):
    n = p_main.shape[0]
    ncp, ncs = tp // CHUNK, ts // CHUNK
    npb = bp * ncp
    nb = n // CHUNK
    rows_per_blk = CHUNK // PREV_ROWS

    def seq_s(s):
        return jnp.maximum(s - npb, 0) // ncs

    def col(c):
        return pl.BlockSpec((CHUNK, DN_QK_W), lambda s, c=c: (s, c))

    def prev(c):
        return pl.BlockSpec((PREV_ROWS, DN_QK_W), lambda s, c=c: (jnp.maximum(s * rows_per_blk - 1, 0), c))

    def cprev(c):
        return pl.BlockSpec((None, 1, CONV_W - 1, DN_QK_W), lambda s, c=c: (layer, seq_s(s), 0, c))

    def cw(c):
        return pl.BlockSpec((CONV_W, DN_QK_W), lambda s, c=c: (0, c))

    kern = functools.partial(_dpre_kernel, npb=npb, ncp=ncp, ncs=ncs)
    return pl.pallas_call(
        kern,
        grid=(nb,),
        in_specs=[
            col(0), col(1), col(2), prev(0), prev(1), prev(2), cprev(0), cprev(1), cprev(2), cw(0), cw(1), cw(2),
            pl.BlockSpec((CHUNK, LANES), lambda s: (s, 0)),
            pl.BlockSpec((SUBLANES, LANES), lambda s: (0, 0)),
        ],
        out_specs=[
            pl.BlockSpec((CHUNK, DN_V_W), lambda s: (s, 0)),
            pl.BlockSpec((2 * CHUNK, DN_QK_W), lambda s: (s, 0)),
            pl.BlockSpec((3 * CHUNK, DN_HEADS * CHUNK), lambda s: (s, 0)),
            pl.BlockSpec((SUBLANES, DN_V_W), lambda s: (s, 0)),
        ],
        out_shape=[
            jax.ShapeDtypeStruct((n, DN_V_W), F32),
            jax.ShapeDtypeStruct((2 * n, DN_QK_W), BF16),
            jax.ShapeDtypeStruct((3 * n, DN_HEADS * CHUNK), BF16),
            jax.ShapeDtypeStruct((nb * SUBLANES, DN_V_W), F32),
        ],
        scratch_shapes=[pltpu.VMEM((3, SUBLANES + CHUNK, DN_QK_W), F32), pltpu.VMEM((3, CHUNK, DN_QK_W), F32)],
        compiler_params=_cparams(("parallel",), SMALL_VMEM_LIMIT),
        name="delta_pre",
    )(p_main, p_main, p_main, p_main, p_main, p_main, state_conv, state_conv, state_conv, conv_w, conv_w, conv_w,
      p_ba, ab)


def _drec_kernel(u_ref, wq_ref, lk_ref, eg_ref, z_ref, sp_ref, dng_ref, o_ref, sop_ref, sos_ref, st_ref,
                 *, npb, ncp, ncs):
    s = pl.program_id(0)
    is_p = s < npb
    cidx = jnp.where(is_p, s % ncp, (s - npb) % ncs)
    first = cidx == 0
    last = cidx == jnp.where(is_p, ncp, ncs) - 1

    @pl.when(first)
    def _():
        for h in range(DN_HEADS):
            st_ref[:, h * DN_DV:(h + 1) * DN_DV] = jnp.where(is_p, 0.0, sp_ref[0, h])

    lane = lax.broadcasted_iota(jnp.int32, (1, PAIR_W), 1)
    left = lane < DN_DV
    dng = dng_ref[...]
    for p in range(N_PAIRS):
        psl = slice(p * PAIR_W, (p + 1) * PAIR_W)
        st = st_ref[:, psl]
        st16 = st.astype(BF16)
        zero = jnp.zeros_like(st16)
        sbd = jnp.concatenate([jnp.where(left, st16, zero), jnp.where(left, zero, st16)], axis=0)
        ws = _dot(wq_ref[:, psl], sbd)
        v_new = (u_ref[:, psl] - ws[:CHUNK]).astype(BF16)
        vzero = jnp.zeros_like(v_new)
        vbd = jnp.concatenate([jnp.where(left, v_new, vzero), jnp.where(left, vzero, v_new)], axis=0)
        t = _dot(lk_ref[:, p * 2 * CHUNK:(p + 1) * 2 * CHUNK], vbd)
        o = ws[CHUNK:] + t[:CHUNK]
        st_ref[:, psl] = st * eg_ref[0:1, psl] + t[CHUNK:]
        for hh in (0, 1):
            sl = slice(p * PAIR_W + hh * DN_DV, p * PAIR_W + (hh + 1) * DN_DV)
            oh = o[:, hh * DN_DV:(hh + 1) * DN_DV]
            o_ref[:, sl] = (_rms(oh, dng) * _silu(z_ref[:, sl].astype(F32))).astype(o_ref.dtype)

    @pl.when(last & is_p)
    def _():
        for h in range(DN_HEADS):
            sop_ref[0, h] = st_ref[:, h * DN_DV:(h + 1) * DN_DV]

    @pl.when(last & jnp.logical_not(is_p))
    def _():
        for h in range(DN_HEADS):
            sos_ref[0, h] = st_ref[:, h * DN_DV:(h + 1) * DN_DV]


def _delta_rec(u, wq, lk, eg, p_main, state_delta, layer, dn_g, bp, tp, bs, ts):
    n = u.shape[0]
    ncp, ncs = tp // CHUNK, ts // CHUNK
    npb = bp * ncp
    nb = n // CHUNK
    zblk = COL_Z // DN_V_W

    def seq_s(s):
        return jnp.maximum(s - npb, 0) // ncs

    def seq_p(s):
        return jnp.minimum(s // ncp, bp - 1)

    kern = functools.partial(_drec_kernel, npb=npb, ncp=ncp, ncs=ncs)
    return pl.pallas_call(
        kern,
        grid=(nb,),
        in_specs=[
            pl.BlockSpec((CHUNK, DN_V_W), lambda s: (s, 0)),
            pl.BlockSpec((2 * CHUNK, DN_QK_W), lambda s: (s, 0)),
            pl.BlockSpec((3 * CHUNK, DN_HEADS * CHUNK), lambda s: (s, 0)),
            pl.BlockSpec((SUBLANES, DN_V_W), lambda s: (s, 0)),
            pl.BlockSpec((CHUNK, DN_V_W), lambda s: (s, zblk)),
            pl.BlockSpec((None, 1, DN_HEADS, DN_DK, DN_DV), lambda s: (layer, seq_s(s), 0, 0, 0)),
            pl.BlockSpec((1, DN_DV), lambda s: (0, 0)),
        ],
        out_specs=[
            pl.BlockSpec((CHUNK, DN_V_W), lambda s: (s, 0)),
            pl.BlockSpec((1, DN_HEADS, DN_DK, DN_DV), lambda s: (seq_p(s), 0, 0, 0)),
            pl.BlockSpec((1, DN_HEADS, DN_DK, DN_DV), lambda s: (seq_s(s), 0, 0, 0)),
        ],
        out_shape=[
            jax.ShapeDtypeStruct((n, DN_V_W), BF16),
            jax.ShapeDtypeStruct((bp, DN_HEADS, DN_DK, DN_DV), F32),
            jax.ShapeDtypeStruct((bs, DN_HEADS, DN_DK, DN_DV), F32),
        ],
        scratch_shapes=[pltpu.VMEM((DN_DK, DN_V_W), F32)],
        compiler_params=_cparams(("arbitrary",), SMALL_VMEM_LIMIT),
        name="delta_rec",
    )(u, wq, lk, eg, p_main, state_delta, dn_g)


def _gmlp_kernel(u_ref, v_ref, lg_ref, lb_ref, w_ref, b_ref, gm_ref, vn_ref, *, npb, sample_len):
    s = pl.program_id(0)
    is_s = s >= npb
    gu = _gelu(u_ref[...].astype(F32))
    gv = _gelu(v_ref[...].astype(F32))
    xc = gv - jnp.mean(gv, axis=-1, keepdims=True)
    var = jnp.mean(xc * xc, axis=-1, keepdims=True)
    vn = xc * lax.rsqrt(var + LN_EPS) * lg_ref[...] + lb_ref[...]

    @pl.when(is_s)
    def _():
        vn_ref[...] = vn

    ri = lax.broadcasted_iota(jnp.int32, (GMLP_CHUNK, GMLP_CHUNK), 0)
    ci = lax.broadcasted_iota(jnp.int32, (GMLP_CHUNK, GMLP_CHUNK), 1)
    same_seq = (ri // sample_len) == (ci // sample_len)
    mask = (ri >= ci) & (same_seq | jnp.logical_not(is_s))
    bias = b_ref[0]
    for g in range(GMLP_GROUPS):
        sl = slice(g * GMLP_GROUP_DIM, (g + 1) * GMLP_GROUP_DIM)
        wl = jnp.where(mask, w_ref[0, g], 0.0).astype(BF16)
        mixed = _dot(wl, vn[:, sl].astype(BF16)) + bias[:, g:g + 1]
        gm_ref[:, sl] = (gu[:, sl] * mixed).astype(gm_ref.dtype)


def _gmlp(p_main, ln_g, ln_b, w2, b2, n_prompt, sample_len):
    n = p_main.shape[0]
    nb = n // GMLP_CHUNK
    npb = n_prompt // GMLP_CHUNK
    ublk = COL_U // GMLP_W
    vblk = COL_V // GMLP_W
    kern = functools.partial(_gmlp_kernel, npb=npb, sample_len=sample_len)
    return pl.pallas_call(
        kern,
        grid=(nb,),
        in_specs=[
            pl.BlockSpec((GMLP_CHUNK, GMLP_W), lambda s: (s, ublk)),
            pl.BlockSpec((GMLP_CHUNK, GMLP_W), lambda s: (s, vblk)),
            pl.BlockSpec((1, GMLP_W), lambda s: (0, 0)),
            pl.BlockSpec((1, GMLP_W), lambda s: (0, 0)),
            pl.BlockSpec((1, GMLP_GROUPS, GMLP_CHUNK, GMLP_CHUNK), lambda s: (jnp.where(s >= npb, 1, 0), 0, 0, 0)),
            pl.BlockSpec((1, GMLP_CHUNK, LANES), lambda s: (jnp.where(s >= npb, 1, 0), 0, 0)),
        ],
        out_specs=[
            pl.BlockSpec((GMLP_CHUNK, GMLP_W), lambda s: (s, 0)),
            pl.BlockSpec((GMLP_CHUNK, GMLP_W), lambda s: (jnp.maximum(s - npb, 0), 0)),
        ],
        out_shape=[
            jax.ShapeDtypeStruct((n, GMLP_W), BF16),
            jax.ShapeDtypeStruct((n - n_prompt, GMLP_W), F32),
        ],
        compiler_params=_cparams(("arbitrary",), SMALL_VMEM_LIMIT),
        name="gmlp",
    )(p_main, p_main, ln_g, ln_b, w2, b2)


def _merge_kernel(o_ref, gm_ref, wd_ref, wg_ref, ga_ref, gb_ref, y_ref, wd16_ref, wg16_ref):
    @pl.when(pl.program_id(1) == 0)
    def _():
        wd16_ref[...] = wd_ref[...].astype(BF16)
        wg16_ref[...] = wg_ref[...].astype(BF16)

    a = _dot(o_ref[...], wd16_ref[...])
    b = _dot(gm_ref[...], wg16_ref[...])
    ga = jax.nn.sigmoid(ga_ref[...].astype(F32))
    gb = jax.nn.sigmoid(gb_ref[...].astype(F32))
    y_ref[...] = (ga * a + gb * b).astype(y_ref.dtype)


def _merge(o, gm, w_dn, w_gm, layer, p_main):
    n = o.shape[0]
    tm = _tile(n, 512)
    tn = 512
    ga0, gb0 = COL_GA // tn, COL_GB // tn
    return pl.pallas_call(
        _merge_kernel,
        grid=(D_MODEL // tn, n // tm),
        in_specs=[
            pl.BlockSpec((tm, DN_V_W), lambda j, i: (i, 0)),
            pl.BlockSpec((tm, GMLP_W), lambda j, i: (i, 0)),
            pl.BlockSpec((None, DN_V_W, tn), lambda j, i: (layer, 0, j)),
            pl.BlockSpec((None, GMLP_W, tn), lambda j, i: (layer, 0, j)),
            pl.BlockSpec((tm, tn), lambda j, i: (i, ga0 + j)),
            pl.BlockSpec((tm, tn), lambda j, i: (i, gb0 + j)),
        ],
        out_specs=pl.BlockSpec((tm, tn), lambda j, i: (i, j)),
        out_shape=jax.ShapeDtypeStruct((n, D_MODEL), BF16),
        scratch_shapes=[pltpu.VMEM((DN_V_W, tn), BF16), pltpu.VMEM((GMLP_W, tn), BF16)],
        compiler_params=_cparams(("parallel", "arbitrary")),
        name="merge",
    )(o, gm, w_dn, w_gm, p_main, p_main)


def _outproj_kernel(y_ref, w_ref, *refs, first_tiles):
    *x_refs, o_ref, w16_ref = refs

    @pl.when(pl.program_id(1) == 0)
    def _():
        w16_ref[...] = w_ref[...].astype(BF16)

    o_ref[...] = _pick_rows(x_refs, first_tiles, 1) + _dot(y_ref[...], w16_ref[...])


def _outproj(y, w, layer, x_parts):
    n = y.shape[0]
    tm = _tile(math.gcd(*[p.shape[0] for p in x_parts]), 512)
    tn = 1024
    x_specs, first_tiles = _row_sources(x_parts, (tm, tn), lambda j, i: (i, j))
    return pl.pallas_call(
        functools.partial(_outproj_kernel, first_tiles=first_tiles),
        grid=(D_MODEL // tn, n // tm),
        in_specs=[
            pl.BlockSpec((tm, D_MODEL), lambda j, i: (i, 0)),
            pl.BlockSpec((None, D_MODEL, tn), lambda j, i: (layer, 0, j)),
        ] + x_specs,
        out_specs=pl.BlockSpec((tm, tn), lambda j, i: (i, j)),
        out_shape=jax.ShapeDtypeStruct((n, D_MODEL), F32),
        scratch_shapes=[pltpu.VMEM((D_MODEL, tn), BF16)],
        compiler_params=_cparams(("parallel", "arbitrary")),
        name="out_proj",
    )(y, w, *x_parts)


def _ffn_kernel(x_ref, g_ref, wg_ref, wu_ref, wd_ref, o_ref, hn_ref):
    @pl.when(pl.program_id(1) == 0)
    def _():
        x = x_ref[...]
        hn_ref[...] = _rms(x, g_ref[...]).astype(BF16)
        o_ref[...] = x

    hn = hn_ref[...]
    hid = (_silu(_dot(hn, wg_ref[...])) * _dot(hn, wu_ref[...])).astype(BF16)
    o_ref[...] += _dot(hid, wd_ref[...])


def _ffn(x, g, wg, wu, wd):
    n = x.shape[0]
    tm = _tile(n, 1024)
    tf = 512
    return pl.pallas_call(
        _ffn_kernel,
        grid=(n // tm, D_FF // tf),
        in_specs=[
            pl.BlockSpec((tm, D_MODEL), lambda i, j: (i, 0)),
            pl.BlockSpec((1, D_MODEL), lambda i, j: (0, 0)),
            pl.BlockSpec((D_MODEL, tf), lambda i, j: (0, j)),
            pl.BlockSpec((D_MODEL, tf), lambda i, j: (0, j)),
            pl.BlockSpec((tf, D_MODEL), lambda i, j: (j, 0)),
        ],
        out_specs=pl.BlockSpec((tm, D_MODEL), lambda i, j: (i, 0)),
        out_shape=jax.ShapeDtypeStruct((n, D_MODEL), F32),
        scratch_shapes=[pltpu.VMEM((tm, D_MODEL), BF16)],
        compiler_params=_cparams(("parallel", "arbitrary")),
        name="ffn_dense",
    )(x, g, wg, wu, wd)


def _router_kernel(x_ref, g_ref, rw_ref, rb_ref, meta_ref, cnt_ref, carry_ref):
    @pl.when(pl.program_id(0) == 0)
    def _():
        carry_ref[...] = jnp.zeros_like(carry_ref)

    tm = x_ref.shape[0]
    hn = _rms(x_ref[...], g_ref[...])
    logits = jnp.dot(hn, rw_ref[...], precision=HIGHEST, preferred_element_type=F32) + rb_ref[...]
    lane = lax.broadcasted_iota(jnp.int32, (tm, LANES), 1).astype(F32)
    m1 = jnp.max(logits, axis=-1, keepdims=True)
    i1 = jnp.min(jnp.where(logits == m1, lane, float(LANES)), axis=-1, keepdims=True)
    oh1 = lane == i1
    rest = jnp.where(oh1, -jnp.inf, logits)
    m2 = jnp.max(rest, axis=-1, keepdims=True)
    i2 = jnp.min(jnp.where(rest == m2, lane, float(LANES)), axis=-1, keepdims=True)
    oh2 = lane == i2
    e = jnp.exp(m2 - m1)
    p1 = 1.0 / (1.0 + e)
    p2 = e / (1.0 + e)

    onehot = jnp.where(oh1 | oh2, 1.0, 0.0)
    ri = lax.broadcasted_iota(jnp.int32, (tm, tm), 0)
    ci = lax.broadcasted_iota(jnp.int32, (tm, tm), 1)
    before = jnp.where(ri > ci, 1.0, 0.0).astype(BF16)
    rank = _dot(before, onehot.astype(BF16)) + carry_ref[...]
    r1 = jnp.sum(jnp.where(oh1, rank, 0.0), axis=-1, keepdims=True)
    r2 = jnp.sum(jnp.where(oh2, rank, 0.0), axis=-1, keepdims=True)
    carry_ref[...] += jnp.sum(onehot, axis=0, keepdims=True)

    meta = jnp.zeros((tm, LANES), F32)
    for idx, val in enumerate((i1, i2, p1, p2, r1, r2)):
        meta = jnp.where(lane == float(idx), val, meta)
    meta_ref[...] = meta
    cnt_ref[...] = carry_ref[...]


def _router(x, g, rw, rb):
    n = x.shape[0]
    tm = _tile(n, 256)
    return pl.pallas_call(
        _router_kernel,
        grid=(n // tm,),
        in_specs=[
            pl.BlockSpec((tm, D_MODEL), lambda i: (i, 0)),
            pl.BlockSpec((1, D_MODEL), lambda i: (0, 0)),
            pl.BlockSpec((D_MODEL, LANES), lambda i: (0, 0)),
            pl.BlockSpec((1, LANES), lambda i: (0, 0)),
        ],
        out_specs=[
            pl.BlockSpec((tm, LANES), lambda i: (i, 0)),
            pl.BlockSpec((1, LANES), lambda i: (0, 0)),
        ],
        out_shape=[jax.ShapeDtypeStruct((n, LANES), F32), jax.ShapeDtypeStruct((1, LANES), F32)],
        scratch_shapes=[pltpu.VMEM((1, LANES), F32)],
        compiler_params=_cparams(("arbitrary",), SMALL_VMEM_LIMIT),
        name="moe_router",
    )(x, g, rw, rb)


DMA_UNROLL = 8


def _gather_tile(idx_ref, idx_base, rows, src_ref, dst_ref, dst_base, sem, *, wait):
    def body(r, carry):
        cp = pltpu.make_async_copy(src_ref.at[idx_ref[idx_base + r]], dst_ref.at[dst_base + r], sem)
        if wait:
            cp.wait()
        else:
            cp.start()
        return carry
    lax.fori_loop(0, rows, body, 0, unroll=DMA_UNROLL)


def _cm_stride(rows):
    return rows + SUBLANES


def _rows_to_chunk_major(rows_ref, cm_ref, rows, stride):
    def body(r, carry):
        cm_ref[pl.ds(r, ROW_CHUNKS, stride=stride), :] = rows_ref[r]
        return carry
    lax.fori_loop(0, rows, body, 0, unroll=DMA_UNROLL)


def _chunk_major_to_rows(cm_ref, rows_ref, rows, stride):
    def body(r, carry):
        rows_ref[r] = cm_ref[pl.ds(r, ROW_CHUNKS, stride=stride), :]
        return carry
    lax.fori_loop(0, rows, body, 0, unroll=DMA_UNROLL)


def _expert_kernel(src_ref, te_ref, na_ref, x3_ref, g_ref, wg_ref, wu_ref, wd_ref, o_ref, buf_ref, hn_ref, cm_ref,
                   sem):
    i = pl.program_id(0)
    j = pl.program_id(1)
    nj = pl.num_programs(1)
    rows = hn_ref.shape[0]
    stride = cm_ref.shape[0] // ROW_CHUNKS
    n_active = na_ref[0]
    active = i < n_active
    slot = i % 2

    def chunk(c):
        return pl.ds(c * stride, rows)

    @pl.when((j == 0) & active)
    def _():
        @pl.when(i == 0)
        def _():
            _gather_tile(src_ref, 0, rows, x3_ref, buf_ref.at[0], 0, sem.at[0], wait=False)

        @pl.when(i + 1 < n_active)
        def _():
            _gather_tile(src_ref, (i + 1) * rows, rows, x3_ref, buf_ref.at[1 - slot], 0, sem.at[1 - slot], wait=False)

        _gather_tile(src_ref, i * rows, rows, x3_ref, buf_ref.at[slot], 0, sem.at[slot], wait=True)
        _rows_to_chunk_major(buf_ref.at[slot], cm_ref, rows, stride)

        ss = jnp.zeros((rows, 1), F32)
        for c in range(ROW_CHUNKS):
            xc = cm_ref[chunk(c), :]
            ss = ss + jnp.sum(xc * xc, axis=-1, keepdims=True)
        scale = lax.rsqrt(ss / D_MODEL + RMS_EPS)
        for c in range(ROW_CHUNKS):
            csl = slice(c * LANES, (c + 1) * LANES)
            hn_ref[:, csl] = (cm_ref[chunk(c), :] * scale * g_ref[:, csl]).astype(BF16)

    @pl.when(j == 0)
    def _():
        cm_ref[...] = jnp.zeros_like(cm_ref)

    @pl.when(active)
    def _():
        hn = hn_ref[...]
        hid = (_silu(_dot(hn, wg_ref[0])) * _dot(hn, wu_ref[0])).astype(BF16)
        out = _dot(hid, wd_ref[0])
        for c in range(ROW_CHUNKS):
            cm_ref[chunk(c), :] += out[:, c * LANES:(c + 1) * LANES]

    @pl.when(j == nj - 1)
    def _():
        _chunk_major_to_rows(cm_ref, o_ref, rows, stride)


def _experts(x3, src, g, wg, wu, wd, tile_expert, n_active, te_rows):
    s_max = src.shape[0]
    n_tiles = s_max // te_rows
    tf = 256
    nj = D_FF_EXPERT // tf

    def row(i, na):
        return jnp.minimum(i, na[0] - 1)

    def jj(i, j, na):
        return jnp.where(i < na[0], j, nj - 1)

    return pl.pallas_call(
        _expert_kernel,
        grid_spec=pltpu.PrefetchScalarGridSpec(
            num_scalar_prefetch=3,
            grid=(n_tiles, nj),
            in_specs=[
                pl.BlockSpec(memory_space=pl.ANY),
                pl.BlockSpec((1, D_MODEL), lambda i, j, sr, te, na: (0, 0)),
                pl.BlockSpec((1, D_MODEL, tf), lambda i, j, sr, te, na: (te[row(i, na)], 0, jj(i, j, na))),
                pl.BlockSpec((1, D_MODEL, tf), lambda i, j, sr, te, na: (te[row(i, na)], 0, jj(i, j, na))),
                pl.BlockSpec((1, tf, D_MODEL), lambda i, j, sr, te, na: (te[row(i, na)], jj(i, j, na), 0)),
            ],
            out_specs=pl.BlockSpec((te_rows, ROW_CHUNKS, LANES), lambda i, j, sr, te, na: (i, 0, 0)),
            scratch_shapes=[
                pltpu.VMEM((2, te_rows, ROW_CHUNKS, LANES), F32),
                pltpu.VMEM((te_rows, D_MODEL), BF16),
                pltpu.VMEM((ROW_CHUNKS * _cm_stride(te_rows), LANES), F32),
                pltpu.SemaphoreType.DMA((2,)),
            ],
        ),
        out_shape=jax.ShapeDtypeStruct((s_max, ROW_CHUNKS, LANES), F32),
        compiler_params=_cparams(("arbitrary", "arbitrary")),
        name="moe_experts",
    )(src, tile_expert, n_active, x3, g, wg, wu, wd)


def _combine_kernel(back_ref, ys3_ref, x_ref, meta_ref, g_ref, y_ref, buf_ref, cm_ref, sem, *, tile0, n_tok):
    i = pl.program_id(0)
    nt = pl.num_programs(0)
    tm = x_ref.shape[0]
    stride = cm_ref.shape[0] // ROW_CHUNKS
    slot = i % 2

    def gather(tile, slt, wait):
        for kk in range(TOP_K):
            _gather_tile(back_ref, kk * n_tok + (tile0 + tile) * tm, tm, ys3_ref, buf_ref.at[slt], kk * tm,
                         sem.at[slt], wait=wait)

    @pl.when(i == 0)
    def _():
        gather(0, 0, False)

    @pl.when(i + 1 < nt)
    def _():
        gather(i + 1, 1 - slot, False)

    gather(i, slot, True)
    _rows_to_chunk_major(buf_ref.at[slot], cm_ref, TOP_K * tm, stride)

    meta = meta_ref[...]
    p1 = meta[:, 2:3]
    p2 = meta[:, 3:4]
    ss = jnp.zeros((tm, 1), F32)
    for c in range(ROW_CHUNKS):
        csl = slice(c * LANES, (c + 1) * LANES)
        e1 = cm_ref[pl.ds(c * stride, tm), :]
        e2 = cm_ref[pl.ds(c * stride + tm, tm), :]
        xn = x_ref[:, csl] + (p1 * e1 + p2 * e2)
        ss = ss + jnp.sum(xn * xn, axis=-1, keepdims=True)
        y_ref[:, csl] = xn
    y_ref[...] = y_ref[...] * lax.rsqrt(ss / D_MODEL + RMS_EPS) * g_ref[...]


def _combine(x, ys3, back, meta, g, row0, rows):
    n = x.shape[0]
    tm = _tile(math.gcd(math.gcd(row0, rows), n), 256)
    b0 = row0 // tm
    kern = functools.partial(_combine_kernel, tile0=b0, n_tok=n)
    return pl.pallas_call(
        kern,
        grid_spec=pltpu.PrefetchScalarGridSpec(
            num_scalar_prefetch=1,
            grid=(rows // tm,),
            in_specs=[
                pl.BlockSpec(memory_space=pl.ANY),
                pl.BlockSpec((tm, D_MODEL), lambda i, bk: (b0 + i, 0)),
                pl.BlockSpec((tm, LANES), lambda i, bk: (b0 + i, 0)),
                pl.BlockSpec((1, D_MODEL), lambda i, bk: (0, 0)),
            ],
            out_specs=pl.BlockSpec((tm, D_MODEL), lambda i, bk: (i, 0)),
            scratch_shapes=[
                pltpu.VMEM((2, TOP_K * tm, ROW_CHUNKS, LANES), F32),
                pltpu.VMEM((ROW_CHUNKS * _cm_stride(TOP_K * tm), LANES), F32),
                pltpu.SemaphoreType.DMA((2,)),
            ],
        ),
        out_shape=jax.ShapeDtypeStruct((rows, D_MODEL), F32),
        compiler_params=_cparams(("arbitrary",), SMALL_VMEM_LIMIT),
        name="moe_combine",
    )(back, ys3, x, meta, g)


EXPERT_TILE = 512


def _routing_tables(meta, counts, n):
    te = EXPERT_TILE
    i1 = meta[:, 0].astype(jnp.int32)
    i2 = meta[:, 1].astype(jnp.int32)
    r1 = meta[:, 4].astype(jnp.int32)
    r2 = meta[:, 5].astype(jnp.int32)
    cnt = counts[0, :N_EXPERTS].astype(jnp.int32)
    padded = (cnt + te - 1) // te * te
    ends = jnp.cumsum(padded)
    offs = ends - padded
    slot1 = offs[i1] + r1
    slot2 = offs[i2] + r2
    s_max = (TOP_K * n + N_EXPERTS * (te - 1) + te - 1) // te * te
    tok = jnp.arange(n, dtype=jnp.int32)
    src = jnp.zeros((s_max,), jnp.int32).at[slot1].set(tok).at[slot2].set(tok)
    n_active = (ends[-1] // te).astype(jnp.int32).reshape(1)
    starts = jnp.arange(s_max // te, dtype=jnp.int32) * te
    tile_expert = jnp.minimum(jnp.searchsorted(ends, starts, side="right"), N_EXPERTS - 1).astype(jnp.int32)
    return src, jnp.concatenate([slot1, slot2]), tile_expert, n_active


def _prep_decay_params(a_log, dt_bias):
    rows = jnp.zeros((SUBLANES, LANES), F32)
    rows = rows.at[0, DN_HEADS:2 * DN_HEADS].set(a_log)
    rows = rows.at[1, DN_HEADS:2 * DN_HEADS].set(dt_bias)
    return rows


def _prep_gmlp_params(sp_w, sp_b, sample_len):
    reps = GMLP_CHUNK // sample_len
    w_s = jnp.tile(sp_w[:, :sample_len, :sample_len], (1, reps, reps))
    b_s = jnp.tile(sp_b[:, :sample_len], (1, reps))
    w2 = jnp.stack([sp_w, w_s])
    b2 = jnp.stack([sp_b.T, b_s.T])
    b2 = jnp.pad(b2, ((0, 0), (0, 0), (0, LANES - GMLP_GROUPS)))
    return w2, b2


def _last_rows(p_main, row0, b, t):
    rows = (row0 + (jnp.arange(b, dtype=jnp.int32) * t + t - (CONV_W - 1))[:, None]
            + jnp.arange(CONV_W - 1, dtype=jnp.int32)[None, :])
    return jnp.take(p_main, rows.reshape(-1), axis=0)[:, :DN_QKV_W].astype(F32).reshape(b, CONV_W - 1, DN_QKV_W)


def kernel(x_prompt, x_sample, state_conv, state_delta, norm_mix_g, w_in, conv_w, a_log, dt_bias, dn_norm_g,
           gm_ln_g, gm_ln_b, sp_w, sp_b, w_dn_out, w_gm_out, w_out, norm_ffn_g, ffn_wg, ffn_wu, ffn_wd,
           router_w, router_b, moe_wg, moe_wu, moe_wd, final_g):
    bp, tp, _ = x_prompt.shape
    bs, ts, _ = x_sample.shape
    n_p, n_s = bp * tp, bs * ts
    n = n_p + n_s
    assert tp % GMLP_CHUNK == 0 and GMLP_CHUNK % ts == 0 and n_s % GMLP_CHUNK == 0 and ts % CHUNK == 0

    x_parts = [x_prompt.reshape(n_p, D_MODEL), x_sample.reshape(n_s, D_MODEL)]
    w_in_t = jnp.swapaxes(w_in, 1, 2)
    conv_p, conv_s, delta_p, delta_s, v_s = [], [], [], [], []
    y_p = y_s = None
    for l in range(DEPTH):
        w_ba_t = jnp.pad(w_in_t[l, OFF_B:OFF_U, :], ((0, LANES - BA_W), (0, 0)))
        hn, p_ba = _mixnorm(x_parts, norm_mix_g[l].reshape(1, D_MODEL), w_ba_t)
        p_main = _inproj(hn, w_in_t, l)
        u, wq, lk, eg = _delta_pre(p_main, p_ba, state_conv, l, conv_w[l], _prep_decay_params(a_log[l], dt_bias[l]),
                                   bp, tp, bs, ts)
        o, sp_out, ss_out = _delta_rec(u, wq, lk, eg, p_main, state_delta, l, dn_norm_g[l].reshape(1, DN_DV),
                                       bp, tp, bs, ts)
        w2, b2 = _prep_gmlp_params(sp_w[l], sp_b[l], ts)
        gm, vn_s = _gmlp(p_main, gm_ln_g[l].reshape(1, GMLP_W), gm_ln_b[l].reshape(1, GMLP_W), w2, b2, n_p, ts)
        y = _merge(o, gm, w_dn_out, w_gm_out, l, p_main)
        x = _outproj(y, w_out, l, x_parts)

        conv_p.append(_last_rows(p_main, 0, bp, tp))
        conv_s.append(_last_rows(p_main, n_p, bs, ts))
        delta_p.append(sp_out)
        delta_s.append(ss_out)
        v_s.append(vn_s.reshape(bs, ts, GMLP_W))

        g_ffn = norm_ffn_g[l].reshape(1, D_MODEL)
        if l % 2 == 0:
            x = _ffn(x, g_ffn, ffn_wg[l // 2].astype(BF16), ffn_wu[l // 2].astype(BF16), ffn_wd[l // 2].astype(BF16))
            x_parts = [x]
        else:
            e = l // 2
            rw = jnp.pad(router_w[e], ((0, 0), (0, LANES - N_EXPERTS)))
            rb = jnp.pad(router_b[e], (0, LANES - N_EXPERTS), constant_values=-jnp.inf).reshape(1, LANES)
            meta, counts = _router(x, g_ffn, rw, rb)
            src, back, tile_expert, n_active = _routing_tables(meta, counts, n)
            ys3 = _experts(x.reshape(n, ROW_CHUNKS, LANES), src, g_ffn, moe_wg[e].astype(BF16),
                           moe_wu[e].astype(BF16), moe_wd[e].astype(BF16), tile_expert, n_active, EXPERT_TILE)
            if l == DEPTH - 1:
                fg = final_g.reshape(1, D_MODEL)
                y_p = _combine(x, ys3, back, meta, fg, 0, n_p)
                y_s = _combine(x, ys3, back, meta, fg, n_p, n_s)
    return (y_p.reshape(bp, tp, D_MODEL), y_s.reshape(bs, ts, D_MODEL), jnp.stack(conv_p), jnp.stack(delta_p),
            jnp.stack(conv_s), jnp.stack(delta_s), jnp.stack(v_s))
```

```python
import functools
import math

import jax
import jax.numpy as jnp
from jax import lax
from jax.experimental import pallas as pl
from jax.experimental.pallas import tpu as pltpu

F32 = jnp.float32
BF16 = jnp.bfloat16
HIGHEST = lax.Precision.HIGHEST

D_MODEL = 2048
DEPTH = 2
CHUNK = 64
DN_HEADS = 16
DN_DK = 128
DN_DV = 128
DN_QK_W = DN_HEADS * DN_DK
DN_V_W = DN_HEADS * DN_DV
DN_QKV_W = 2 * DN_QK_W + DN_V_W
CONV_W = 4
DN_SCALE = DN_DK ** -0.5
GMLP_CHUNK = 128
GMLP_GROUPS = 16
GMLP_GROUP_DIM = 128
GMLP_W = GMLP_GROUPS * GMLP_GROUP_DIM
OFF_Z = DN_QKV_W
OFF_B = OFF_Z + DN_V_W
OFF_A = OFF_B + DN_HEADS
OFF_U = OFF_A + DN_HEADS
OFF_V = OFF_U + GMLP_W
OFF_GA = OFF_V + GMLP_W
OFF_GB = OFF_GA + D_MODEL
IN_W = OFF_GB + D_MODEL
D_FF = 11 * D_MODEL // 4
N_EXPERTS = 8
TOP_K = 2
D_FF_EXPERT = D_FF // 2
RMS_EPS = 1e-6
LN_EPS = 1e-5
L2_EPS = 1e-6

LANES = 128
SUBLANES = 8
V7X_VMEM_LIMIT = 56 * 1024 * 1024
SMALL_VMEM_LIMIT = 32 * 1024 * 1024

MAIN_W = IN_W - 2 * DN_HEADS
COL_Z = OFF_Z
COL_U = COL_Z + DN_V_W
COL_V = COL_U + GMLP_W
COL_GA = COL_V + GMLP_W
COL_GB = COL_GA + D_MODEL

ROW_CHUNKS = D_MODEL // LANES
N_PAIRS = DN_HEADS // 2
PAIR_W = 2 * DN_DK
PREV_ROWS = 16
PAIR_GROUP = 8


def _tile(n, pref):
    t = pref
    while n % t:
        t //= 2
    return t


def _cparams(sem, vmem_bytes=V7X_VMEM_LIMIT):
    return pltpu.CompilerParams(dimension_semantics=sem, vmem_limit_bytes=vmem_bytes)


def _rms(x, g):
    ms = jnp.mean(x * x, axis=-1, keepdims=True)
    return x * lax.rsqrt(ms + RMS_EPS) * g


def _dot(a, b):
    return jnp.dot(a, b, preferred_element_type=F32)


def _dot_nt(a, b):
    return lax.dot_general(a, b, (((1,), (1,)), ((), ())), preferred_element_type=F32)


def _silu(x):
    return x * jax.nn.sigmoid(x)


def _gelu(x):
    return 0.5 * x * (1.0 + jnp.tanh(math.sqrt(2.0 / math.pi) * (x + 0.044715 * (x * x * x))))


def _softplus(x):
    return jnp.maximum(x, 0.0) + jnp.log1p(jnp.exp(-jnp.abs(x)))


BA_W = 2 * DN_HEADS


def _row_sources(parts, block_shape, index_fn):
    if len(parts) == 1:
        return [pl.BlockSpec(block_shape, index_fn)], 0
    tm = block_shape[0]
    assert len(parts) == 2 and all(p.shape[0] % tm == 0 for p in parts)
    t0 = parts[0].shape[0] // tm

    def first(*ids):
        r, c = index_fn(*ids)
        return jnp.minimum(r, t0 - 1), c

    def second(*ids):
        r, c = index_fn(*ids)
        return jnp.maximum(r - t0, 0), c

    return [pl.BlockSpec(block_shape, first), pl.BlockSpec(block_shape, second)], t0


def _pick_rows(refs, first_tiles, row_axis):
    if len(refs) == 1:
        return refs[0][...]
    return jnp.where(pl.program_id(row_axis) < first_tiles, refs[0][...], refs[1][...])


def _mixnorm_kernel(*refs, first_tiles):
    *x_refs, g_ref, wbat_ref, hn_ref, pba_ref = refs
    hn = _rms(_pick_rows(x_refs, first_tiles, 0), g_ref[...]).astype(BF16)
    hn_ref[...] = hn
    pba_ref[...] = _dot_nt(hn, wbat_ref[...].astype(BF16))


def _mixnorm(x_parts, g, w_ba_t):
    n = sum(p.shape[0] for p in x_parts)
    tm = _tile(math.gcd(*[p.shape[0] for p in x_parts]), 512)
    x_specs, first_tiles = _row_sources(x_parts, (tm, D_MODEL), lambda i: (i, 0))
    return pl.pallas_call(
        functools.partial(_mixnorm_kernel, first_tiles=first_tiles),
        grid=(n // tm,),
        in_specs=x_specs + [
            pl.BlockSpec((1, D_MODEL), lambda i: (0, 0)),
            pl.BlockSpec((LANES, D_MODEL), lambda i: (0, 0)),
        ],
        out_specs=[
            pl.BlockSpec((tm, D_MODEL), lambda i: (i, 0)),
            pl.BlockSpec((tm, LANES), lambda i: (i, 0)),
        ],
        out_shape=[jax.ShapeDtypeStruct((n, D_MODEL), BF16), jax.ShapeDtypeStruct((n, LANES), F32)],
        compiler_params=_cparams(("parallel",), SMALL_VMEM_LIMIT),
        name="mix_norm",
    )(*x_parts, g, w_ba_t)


def _inproj_kernel(hn_ref, wa_ref, wb_ref, p_ref, w16_ref, *, first_shifted):
    j = pl.program_id(0)

    @pl.when(pl.program_id(1) == 0)
    def _():
        @pl.when(j < first_shifted)
        def _():
            w16_ref[...] = wa_ref[...].T.astype(BF16)

        @pl.when(j >= first_shifted)
        def _():
            wt = jnp.concatenate([wa_ref[BA_W:, :], wb_ref[:BA_W, :]], axis=0)
            w16_ref[...] = wt.T.astype(BF16)

    p_ref[...] = _dot(hn_ref[...], w16_ref[...]).astype(p_ref.dtype)


def _inproj(hn, w_in_t, layer):
    n = hn.shape[0]
    tm = _tile(n, 1024)
    tn = 1024
    assert OFF_B % tn == 0 and OFF_U - OFF_B == BA_W and BA_W % SUBLANES == 0
    first_shifted = OFF_B // tn
    tail_blocks = tn // LANES
    kern = functools.partial(_inproj_kernel, first_shifted=first_shifted)
    return pl.pallas_call(
        kern,
        grid=(MAIN_W // tn, n // tm),
        in_specs=[
            pl.BlockSpec((tm, D_MODEL), lambda j, i: (i, 0)),
            pl.BlockSpec((None, tn, D_MODEL), lambda j, i: (layer, j, 0)),
            pl.BlockSpec((None, LANES, D_MODEL), lambda j, i: (layer, (j + 1) * tail_blocks, 0)),
        ],
        out_specs=pl.BlockSpec((tm, tn), lambda j, i: (i, j)),
        out_shape=jax.ShapeDtypeStruct((n, MAIN_W), BF16),
        scratch_shapes=[pltpu.VMEM((D_MODEL, tn), BF16)],
        compiler_params=_cparams(("parallel", "arbitrary")),
        name="in_proj",
    )(hn, w_in_t, w_in_t)


def _dpre_kernel(q_ref, k_ref, v_ref, pq_ref, pk_ref, pv_ref, cq_ref, ck_ref, cv_ref, cwq_ref, cwk_ref, cwv_ref,
                 ba_ref, ab_ref, u_ref, wqo_ref, lk_ref, eg_ref, tq_ref, tk_ref, tv_ref, xbuf_ref, cbuf_ref, cnext_ref,
                 *, nb, npb, ncp, ncs):
    t = pl.program_id(0)
    s = jnp.minimum(t, nb - 1)
    is_p = s < npb
    first = jnp.where(is_p, s % ncp, (s - npb) % ncs) == 0
    hist = SUBLANES - (CONV_W - 1)

    @pl.when(t == 0)
    def _():
        cbuf_ref[...] = jnp.zeros_like(cbuf_ref)

    def conv_silu(j, raw_ref, prev_ref, cp_ref, cw_ref, tail_ref):
        raw = raw_ref[...].astype(F32)
        tail_ref[0] = raw[CHUNK - SUBLANES:, :]
        tail = prev_ref[prev_ref.shape[0] - SUBLANES:, :].astype(F32)
        xbuf_ref[j] = tail
        xbuf_ref[j, hist:SUBLANES, :] = jnp.where(first, jnp.where(is_p, 0.0, cp_ref[0]), tail[hist:, :])
        rows = jnp.concatenate([xbuf_ref[j], raw], axis=0)
        acc = raw * cw_ref[CONV_W - 1:CONV_W, :]
        for sft in range(1, CONV_W):
            acc = acc + pltpu.roll(rows, sft, axis=0)[SUBLANES:] * cw_ref[CONV_W - 1 - sft:CONV_W - sft, :]
        return _silu(acc)

    ba = ba_ref[...]
    lane = lax.broadcasted_iota(jnp.int32, (CHUNK, LANES), 1)
    gval = -jnp.exp(ab_ref[0:1, :]) * _softplus(ba + ab_ref[1:2, :])
    bg = jnp.where(lane < DN_HEADS, jax.nn.sigmoid(ba), gval)
    r64 = lax.broadcasted_iota(jnp.int32, (CHUNK, CHUNK), 0)
    c64 = lax.broadcasted_iota(jnp.int32, (CHUNK, CHUNK), 1)
    gc_cols = jnp.dot((r64 >= c64).astype(F32), bg, precision=HIGHEST, preferred_element_type=F32)
    gc_rows = gc_cols.T
    gc_rows2 = jnp.concatenate([gc_rows, gc_rows], axis=1)

    ri = lax.broadcasted_iota(jnp.int32, (2 * CHUNK, 2 * CHUNK), 0)
    ci = lax.broadcasted_iota(jnp.int32, (2 * CHUNK, 2 * CHUNK), 1)
    same = (ri // CHUNK) == (ci // CHUNK)
    causal = same & (ri >= ci)
    strict = same & (ri > ci)
    left = lax.broadcasted_iota(jnp.int32, (1, 2 * CHUNK), 1) < CHUNK

    def stack(j, p):
        return jnp.concatenate([cbuf_ref[j, :, (2 * p) * DN_DK:(2 * p + 1) * DN_DK],
                                cbuf_ref[j, :, (2 * p + 1) * DN_DK:(2 * p + 2) * DN_DK]], axis=0)

    def colpair(a, off, p):
        return jnp.concatenate([a[:, off + 2 * p:off + 2 * p + 1], a[:, off + 2 * p + 1:off + 2 * p + 2]], axis=0)

    def pair_group(pairs):
        beta = {p: colpair(bg, 0, p) for p in pairs}
        gcc = {p: colpair(gc_cols, DN_HEADS, p) for p in pairs}
        gcr = {p: jnp.where(left, gc_rows2[DN_HEADS + 2 * p:DN_HEADS + 2 * p + 1, :],
                            gc_rows2[DN_HEADS + 2 * p + 1:DN_HEADS + 2 * p + 2, :]) for p in pairs}
        gl = {p: [gc_cols[CHUNK - 1:CHUNK, DN_HEADS + 2 * p + t:DN_HEADS + 2 * p + t + 1] for t in (0, 1)]
              for p in pairs}
        glr = {p: jnp.where(left, gl[p][0], gl[p][1]) for p in pairs}
        decay = {p: jnp.exp(jnp.where(causal, gcc[p] - gcr[p], -jnp.inf)) for p in pairs}
        egc = {p: jnp.exp(gcc[p]) for p in pairs}

        qh = {p: stack(0, p) for p in pairs}
        kh = {p: stack(1, p) for p in pairs}
        q = {p: qh[p] * (lax.rsqrt(jnp.sum(qh[p] * qh[p], axis=-1, keepdims=True) + L2_EPS) * DN_SCALE) for p in pairs}
        k = {p: kh[p] * lax.rsqrt(jnp.sum(kh[p] * kh[p], axis=-1, keepdims=True) + L2_EPS) for p in pairs}
        kb = {p: k[p] * beta[p] for p in pairs}
        k16 = {p: k[p].astype(BF16) for p in pairs}

        m = {p: jnp.where(strict, -(_dot_nt(kb[p].astype(BF16), k16[p]) * decay[p]), 0.0) for p in pairs}
        r = dict(m)
        for _ in range(5):
            m16 = {p: m[p].astype(BF16) for p in pairs}
            m = {p: _dot(m16[p], m16[p]) for p in pairs}
            r = {p: r[p] + m[p] + _dot(r[p].astype(BF16), m[p].astype(BF16)) for p in pairs}
        rhs = {p: jnp.concatenate([stack(2, p) * beta[p], kb[p] * egc[p]], axis=1) for p in pairs}
        uw = {p: rhs[p] + _dot(r[p].astype(BF16), rhs[p].astype(BF16)) for p in pairs}
        qk = {p: _dot_nt(q[p].astype(BF16), k16[p]) * decay[p] for p in pairs}
        kdt = {p: k[p].T * jnp.exp(glr[p] - gcr[p]) for p in pairs}

        for p in pairs:
            for t in (0, 1):
                sl = slice((2 * p + t) * DN_DK, (2 * p + t + 1) * DN_DK)
                rows = slice(t * CHUNK, (t + 1) * CHUNK)
                u_ref[:, sl] = uw[p][rows, :DN_DV]
                wqo_ref[0:CHUNK, sl] = uw[p][rows, DN_DV:].astype(BF16)
                wqo_ref[CHUNK:2 * CHUNK, sl] = (q[p][rows] * egc[p][rows]).astype(BF16)
                eg_ref[:, sl] = jnp.broadcast_to(jnp.exp(gl[p][t]), (SUBLANES, DN_DV))
            psl = slice(p * 2 * CHUNK, (p + 1) * 2 * CHUNK)
            lk_ref[0:CHUNK, psl] = jnp.where(left, qk[p][:CHUNK], qk[p][CHUNK:]).astype(BF16)
            lk_ref[CHUNK:3 * CHUNK, psl] = kdt[p].astype(BF16)

    for first_pair in range(0, N_PAIRS, PAIR_GROUP):
        pair_group(range(first_pair, first_pair + PAIR_GROUP))

    cnext_ref[0] = conv_silu(0, q_ref, pq_ref, cq_ref, cwq_ref, tq_ref)
    cnext_ref[1] = conv_silu(1, k_ref, pk_ref, ck_ref, cwk_ref, tk_ref)
    cnext_ref[2] = conv_silu(2, v_ref, pv_ref, cv_ref, cwv_ref, tv_ref)
    cbuf_ref[...] = cnext_ref[...]


def _delta_pre(p_main, p_ba, state_conv, layer, conv_w, ab, bp, tp, bs, ts):
    n = p_main.shape[0]
    ncp, ncs = tp // CHUNK, ts // CHUNK
    npb = bp * ncp
    nb = n // CHUNK
    rows_per_blk = CHUNK // PREV_ROWS

    def conv_blk(t):
        return jnp.minimum(t, nb - 1)

    def mat_blk(t):
        return jnp.maximum(t - 1, 0)

    def seq_s(t):
        return jnp.maximum(conv_blk(t) - npb, 0) // ncs

    def seq_all(t):
        blk = conv_blk(t)
        return jnp.where(blk < npb, blk // ncp, bp + (blk - npb) // ncs)

    def col(c):
        return pl.BlockSpec((CHUNK, DN_QK_W), lambda t, c=c: (conv_blk(t), c))

    def prev(c):
        return pl.BlockSpec((PREV_ROWS, DN_QK_W),
                            lambda t, c=c: (jnp.maximum(conv_blk(t) * rows_per_blk - 1, 0), c))

    def cprev(c):
        return pl.BlockSpec((None, 1, CONV_W - 1, DN_QK_W), lambda t, c=c: (layer, seq_s(t), 0, c))

    def cw(c):
        return pl.BlockSpec((CONV_W, DN_QK_W), lambda t, c=c: (0, c))

    kern = functools.partial(_dpre_kernel, nb=nb, npb=npb, ncp=ncp, ncs=ncs)
    return pl.pallas_call(
        kern,
        grid=(nb + 1,),
        in_specs=[
            col(0), col(1), col(2), prev(0), prev(1), prev(2), cprev(0), cprev(1), cprev(2), cw(0), cw(1), cw(2),
            pl.BlockSpec((CHUNK, LANES), lambda t: (mat_blk(t), 0)),
            pl.BlockSpec((SUBLANES, LANES), lambda t: (0, 0)),
        ],
        out_specs=[
            pl.BlockSpec((CHUNK, DN_V_W), lambda t: (mat_blk(t), 0)),
            pl.BlockSpec((2 * CHUNK, DN_QK_W), lambda t: (mat_blk(t), 0)),
            pl.BlockSpec((3 * CHUNK, DN_HEADS * CHUNK), lambda t: (mat_blk(t), 0)),
            pl.BlockSpec((SUBLANES, DN_V_W), lambda t: (mat_blk(t), 0)),
        ] + [pl.BlockSpec((1, SUBLANES, DN_QK_W), lambda t: (seq_all(t), 0, 0))] * 3,
        out_shape=[
            jax.ShapeDtypeStruct((n, DN_V_W), F32),
            jax.ShapeDtypeStruct((2 * n, DN_QK_W), BF16),
            jax.ShapeDtypeStruct((3 * n, DN_HEADS * CHUNK), BF16),
            jax.ShapeDtypeStruct((nb * SUBLANES, DN_V_W), F32),
        ] + [jax.ShapeDtypeStruct((bp + bs, SUBLANES, DN_QK_W), F32)] * 3,
        scratch_shapes=[pltpu.VMEM((3, SUBLANES, DN_QK_W), F32), pltpu.VMEM((3, CHUNK, DN_QK_W), F32),
                        pltpu.VMEM((3, CHUNK, DN_QK_W), F32)],
        compiler_params=_cparams(("arbitrary",), SMALL_VMEM_LIMIT),
        name="delta_pre",
    )(p_main, p_main, p_main, p_main, p_main, p_main, state_conv, state_conv, state_conv, conv_w, conv_w, conv_w,
      p_ba, ab)


def _drec_kernel(u_ref, wq_ref, lk_ref, eg_ref, z_ref, sp_ref, dng_ref, o_ref, sop_ref, sos_ref, st_ref,
                 *, npb, ncp, ncs):
    s = pl.program_id(0)
    is_p = s < npb
    cidx = jnp.where(is_p, s % ncp, (s - npb) % ncs)
    first = cidx == 0
    last = cidx == jnp.where(is_p, ncp, ncs) - 1

    @pl.when(first)
    def _():
        for h in range(DN_HEADS):
            st_ref[:, h * DN_DV:(h + 1) * DN_DV] = jnp.where(is_p, 0.0, sp_ref[0, h])

    lane = lax.broadcasted_iota(jnp.int32, (1, PAIR_W), 1)
    left = lane < DN_DV
    dng = dng_ref[...]
    for p in range(N_PAIRS):
        psl = slice(p * PAIR_W, (p + 1) * PAIR_W)
        st = st_ref[:, psl]
        st16 = st.astype(BF16)
        zero = jnp.zeros_like(st16)
        sbd = jnp.concatenate([jnp.where(left, st16, zero), jnp.where(left, zero, st16)], axis=0)
        ws = _dot(wq_ref[:, psl], sbd)
        v_new = (u_ref[:, psl] - ws[:CHUNK]).astype(BF16)
        vzero = jnp.zeros_like(v_new)
        vbd = jnp.concatenate([jnp.where(left, v_new, vzero), jnp.where(left, vzero, v_new)], axis=0)
        t = _dot(lk_ref[:, p * 2 * CHUNK:(p + 1) * 2 * CHUNK], vbd)
        o = ws[CHUNK:] + t[:CHUNK]
        st_ref[:, psl] = st * eg_ref[0:1, psl] + t[CHUNK:]
        for hh in (0, 1):
            sl = slice(p * PAIR_W + hh * DN_DV, p * PAIR_W + (hh + 1) * DN_DV)
            oh = o[:, hh * DN_DV:(hh + 1) * DN_DV]
            o_ref[:, sl] = (_rms(oh, dng) * _silu(z_ref[:, sl].astype(F32))).astype(o_ref.dtype)

    @pl.when(last & is_p)
    def _():
        for h in range(DN_HEADS):
            sop_ref[0, h] = st_ref[:, h * DN_DV:(h + 1) * DN_DV]

    @pl.when(last & jnp.logical_not(is_p))
    def _():
        for h in range(DN_HEADS):
            sos_ref[0, h] = st_ref[:, h * DN_DV:(h + 1) * DN_DV]


def _delta_rec(u, wq, lk, eg, p_main, state_delta, layer, dn_g, bp, tp, bs, ts):
    n = u.shape[0]
    ncp, ncs = tp // CHUNK, ts // CHUNK
    npb = bp * ncp
    nb = n // CHUNK
    zblk = COL_Z // DN_V_W

    def seq_s(s):
        return jnp.maximum(s - npb, 0) // ncs

    def seq_p(s):
        return jnp.minimum(s // ncp, bp - 1)

    kern = functools.partial(_drec_kernel, npb=npb, ncp=ncp, ncs=ncs)
    return pl.pallas_call(
        kern,
        grid=(nb,),
        in_specs=[
            pl.BlockSpec((CHUNK, DN_V_W), lambda s: (s, 0)),
            pl.BlockSpec((2 * CHUNK, DN_QK_W), lambda s: (s, 0)),
            pl.BlockSpec((3 * CHUNK, DN_HEADS * CHUNK), lambda s: (s, 0)),
            pl.BlockSpec((SUBLANES, DN_V_W), lambda s: (s, 0)),
            pl.BlockSpec((CHUNK, DN_V_W), lambda s: (s, zblk)),
            pl.BlockSpec((None, 1, DN_HEADS, DN_DK, DN_DV), lambda s: (layer, seq_s(s), 0, 0, 0)),
            pl.BlockSpec((1, DN_DV), lambda s: (0, 0)),
        ],
        out_specs=[
            pl.BlockSpec((CHUNK, DN_V_W), lambda s: (s, 0)),
            pl.BlockSpec((1, DN_HEADS, DN_DK, DN_DV), lambda s: (seq_p(s), 0, 0, 0)),
            pl.BlockSpec((1, DN_HEADS, DN_DK, DN_DV), lambda s: (seq_s(s), 0, 0, 0)),
        ],
        out_shape=[
            jax.ShapeDtypeStruct((n, DN_V_W), BF16),
            jax.ShapeDtypeStruct((bp, DN_HEADS, DN_DK, DN_DV), F32),
            jax.ShapeDtypeStruct((bs, DN_HEADS, DN_DK, DN_DV), F32),
        ],
        scratch_shapes=[pltpu.VMEM((DN_DK, DN_V_W), F32)],
        compiler_params=_cparams(("arbitrary",), SMALL_VMEM_LIMIT),
        name="delta_rec",
    )(u, wq, lk, eg, p_main, state_delta, dn_g)


def _gmlp_kernel(u_ref, v_ref, lg_ref, lb_ref, w_ref, b_ref, gm_ref, vn_ref, *, npb, sample_len):
    s = pl.program_id(0)
    is_s = s >= npb
    gu = _gelu(u_ref[...].astype(F32))
    gv = _gelu(v_ref[...].astype(F32))
    xc = gv - jnp.mean(gv, axis=-1, keepdims=True)
    var = jnp.mean(xc * xc, axis=-1, keepdims=True)
    vn = xc * lax.rsqrt(var + LN_EPS) * lg_ref[...] + lb_ref[...]

    @pl.when(is_s)
    def _():
        vn_ref[...] = vn

    ri = lax.broadcasted_iota(jnp.int32, (GMLP_CHUNK, GMLP_CHUNK), 0)
    ci = lax.broadcasted_iota(jnp.int32, (GMLP_CHUNK, GMLP_CHUNK), 1)
    same_seq = (ri // sample_len) == (ci // sample_len)
    mask = (ri >= ci) & (same_seq | jnp.logical_not(is_s))
    bias = b_ref[0]
    for g in range(GMLP_GROUPS):
        sl = slice(g * GMLP_GROUP_DIM, (g + 1) * GMLP_GROUP_DIM)
        wl = jnp.where(mask, w_ref[0, g], 0.0).astype(BF16)
        mixed = _dot(wl, vn[:, sl].astype(BF16)) + bias[:, g:g + 1]
        gm_ref[:, sl] = (gu[:, sl] * mixed).astype(gm_ref.dtype)


def _gmlp(p_main, ln_g, ln_b, w2, b2, n_prompt, sample_len):
    n = p_main.shape[0]
    nb = n // GMLP_CHUNK
    npb = n_prompt // GMLP_CHUNK
    ublk = COL_U // GMLP_W
    vblk = COL_V // GMLP_W
    kern = functools.partial(_gmlp_kernel, npb=npb, sample_len=sample_len)
    return pl.pallas_call(
        kern,
        grid=(nb,),
        in_specs=[
            pl.BlockSpec((GMLP_CHUNK, GMLP_W), lambda s: (s, ublk)),
            pl.BlockSpec((GMLP_CHUNK, GMLP_W), lambda s: (s, vblk)),
            pl.BlockSpec((1, GMLP_W), lambda s: (0, 0)),
            pl.BlockSpec((1, GMLP_W), lambda s: (0, 0)),
            pl.BlockSpec((1, GMLP_GROUPS, GMLP_CHUNK, GMLP_CHUNK), lambda s: (jnp.where(s >= npb, 1, 0), 0, 0, 0)),
            pl.BlockSpec((1, GMLP_CHUNK, LANES), lambda s: (jnp.where(s >= npb, 1, 0), 0, 0)),
        ],
        out_specs=[
            pl.BlockSpec((GMLP_CHUNK, GMLP_W), lambda s: (s, 0)),
            pl.BlockSpec((GMLP_CHUNK, GMLP_W), lambda s: (jnp.maximum(s - npb, 0), 0)),
        ],
        out_shape=[
            jax.ShapeDtypeStruct((n, GMLP_W), BF16),
            jax.ShapeDtypeStruct((n - n_prompt, GMLP_W), F32),
        ],
        compiler_params=_cparams(("arbitrary",), SMALL_VMEM_LIMIT),
        name="gmlp",
    )(p_main, p_main, ln_g, ln_b, w2, b2)


def _merge_kernel(o_ref, gm_ref, wd_ref, wg_ref, ga_ref, gb_ref, y_ref, wd16_ref, wg16_ref):
    @pl.when(pl.program_id(1) == 0)
    def _():
        wd16_ref[...] = wd_ref[...].astype(BF16)
        wg16_ref[...] = wg_ref[...].astype(BF16)

    a = _dot(o_ref[...], wd16_ref[...])
    b = _dot(gm_ref[...], wg16_ref[...])
    ga = jax.nn.sigmoid(ga_ref[...].astype(F32))
    gb = jax.nn.sigmoid(gb_ref[...].astype(F32))
    y_ref[...] = (ga * a + gb * b).astype(y_ref.dtype)


def _merge(o, gm, w_dn, w_gm, layer, p_main):
    n = o.shape[0]
    tm = _tile(n, 256)
    tn = 1024
    ga0, gb0 = COL_GA // tn, COL_GB // tn
    return pl.pallas_call(
        _merge_kernel,
        grid=(D_MODEL // tn, n // tm),
        in_specs=[
            pl.BlockSpec((tm, DN_V_W), lambda j, i: (i, 0)),
            pl.BlockSpec((tm, GMLP_W), lambda j, i: (i, 0)),
            pl.BlockSpec((None, DN_V_W, tn), lambda j, i: (layer, 0, j)),
            pl.BlockSpec((None, GMLP_W, tn), lambda j, i: (layer, 0, j)),
            pl.BlockSpec((tm, tn), lambda j, i: (i, ga0 + j)),
            pl.BlockSpec((tm, tn), lambda j, i: (i, gb0 + j)),
        ],
        out_specs=pl.BlockSpec((tm, tn), lambda j, i: (i, j)),
        out_shape=jax.ShapeDtypeStruct((n, D_MODEL), BF16),
        scratch_shapes=[pltpu.VMEM((DN_V_W, tn), BF16), pltpu.VMEM((GMLP_W, tn), BF16)],
        compiler_params=_cparams(("parallel", "arbitrary")),
        name="merge",
    )(o, gm, w_dn, w_gm, p_main, p_main)


def _outproj_kernel(y_ref, w_ref, *refs, first_tiles):
    *x_refs, o_ref, w16_ref = refs

    @pl.when(pl.program_id(1) == 0)
    def _():
        w16_ref[...] = w_ref[...].astype(BF16)

    o_ref[...] = _pick_rows(x_refs, first_tiles, 1) + _dot(y_ref[...], w16_ref[...])


def _outproj(y, w, layer, x_parts):
    n = y.shape[0]
    tm = _tile(math.gcd(*[p.shape[0] for p in x_parts]), 512)
    tn = 1024
    x_specs, first_tiles = _row_sources(x_parts, (tm, tn), lambda j, i: (i, j))
    return pl.pallas_call(
        functools.partial(_outproj_kernel, first_tiles=first_tiles),
        grid=(D_MODEL // tn, n // tm),
        in_specs=[
            pl.BlockSpec((tm, D_MODEL), lambda j, i: (i, 0)),
            pl.BlockSpec((None, D_MODEL, tn), lambda j, i: (layer, 0, j)),
        ] + x_specs,
        out_specs=pl.BlockSpec((tm, tn), lambda j, i: (i, j)),
        out_shape=jax.ShapeDtypeStruct((n, D_MODEL), F32),
        scratch_shapes=[pltpu.VMEM((D_MODEL, tn), BF16)],
        compiler_params=_cparams(("parallel", "arbitrary")),
        name="out_proj",
    )(y, w, *x_parts)


def _ffn_kernel(x_ref, g_ref, wg_ref, wu_ref, wd_ref, o_ref, hn_ref):
    @pl.when(pl.program_id(1) == 0)
    def _():
        x = x_ref[...]
        hn_ref[...] = _rms(x, g_ref[...]).astype(BF16)
        o_ref[...] = x

    hn = hn_ref[...]
    hid = (_silu(_dot(hn, wg_ref[...])) * _dot(hn, wu_ref[...])).astype(BF16)
    o_ref[...] += _dot(hid, wd_ref[...])


def _ffn(x, g, wg, wu, wd):
    n = x.shape[0]
    tm = _tile(n, 1024)
    tf = 512
    return pl.pallas_call(
        _ffn_kernel,
        grid=(n // tm, D_FF // tf),
        in_specs=[
            pl.BlockSpec((tm, D_MODEL), lambda i, j: (i, 0)),
            pl.BlockSpec((1, D_MODEL), lambda i, j: (0, 0)),
            pl.BlockSpec((D_MODEL, tf), lambda i, j: (0, j)),
            pl.BlockSpec((D_MODEL, tf), lambda i, j: (0, j)),
            pl.BlockSpec((tf, D_MODEL), lambda i, j: (j, 0)),
        ],
        out_specs=pl.BlockSpec((tm, D_MODEL), lambda i, j: (i, 0)),
        out_shape=jax.ShapeDtypeStruct((n, D_MODEL), F32),
        scratch_shapes=[pltpu.VMEM((tm, D_MODEL), BF16)],
        compiler_params=_cparams(("parallel", "arbitrary")),
        name="ffn_dense",
    )(x, g, wg, wu, wd)


def _router_kernel(x_ref, g_ref, rw_ref, rb_ref, meta_ref, cnt_ref, carry_ref):
    @pl.when(pl.program_id(0) == 0)
    def _():
        carry_ref[...] = jnp.zeros_like(carry_ref)

    tm = x_ref.shape[0]
    hn = _rms(x_ref[...], g_ref[...])
    logits = jnp.dot(hn, rw_ref[...], precision=HIGHEST, preferred_element_type=F32) + rb_ref[...]
    lane = lax.broadcasted_iota(jnp.int32, (tm, LANES), 1).astype(F32)
    m1 = jnp.max(logits, axis=-1, keepdims=True)
    i1 = jnp.min(jnp.where(logits == m1, lane, float(LANES)), axis=-1, keepdims=True)
    oh1 = lane == i1
    rest = jnp.where(oh1, -jnp.inf, logits)
    m2 = jnp.max(rest, axis=-1, keepdims=True)
    i2 = jnp.min(jnp.where(rest == m2, lane, float(LANES)), axis=-1, keepdims=True)
    oh2 = lane == i2
    e = jnp.exp(m2 - m1)
    p1 = 1.0 / (1.0 + e)
    p2 = e / (1.0 + e)

    onehot = jnp.where(oh1 | oh2, 1.0, 0.0)
    ri = lax.broadcasted_iota(jnp.int32, (tm, tm), 0)
    ci = lax.broadcasted_iota(jnp.int32, (tm, tm), 1)
    before = jnp.where(ri > ci, 1.0, 0.0).astype(BF16)
    rank = _dot(before, onehot.astype(BF16)) + carry_ref[...]
    r1 = jnp.sum(jnp.where(oh1, rank, 0.0), axis=-1, keepdims=True)
    r2 = jnp.sum(jnp.where(oh2, rank, 0.0), axis=-1, keepdims=True)
    carry_ref[...] += jnp.sum(onehot, axis=0, keepdims=True)

    meta = jnp.zeros((tm, LANES), F32)
    for idx, val in enumerate((i1, i2, p1, p2, r1, r2)):
        meta = jnp.where(lane == float(idx), val, meta)
    meta_ref[...] = meta
    cnt_ref[...] = carry_ref[...]


def _router(x, g, rw, rb):
    n = x.shape[0]
    tm = _tile(n, 256)
    return pl.pallas_call(
        _router_kernel,
        grid=(n // tm,),
        in_specs=[
            pl.BlockSpec((tm, D_MODEL), lambda i: (i, 0)),
            pl.BlockSpec((1, D_MODEL), lambda i: (0, 0)),
            pl.BlockSpec((D_MODEL, LANES), lambda i: (0, 0)),
            pl.BlockSpec((1, LANES), lambda i: (0, 0)),
        ],
        out_specs=[
            pl.BlockSpec((tm, LANES), lambda i: (i, 0)),
            pl.BlockSpec((1, LANES), lambda i: (0, 0)),
        ],
        out_shape=[jax.ShapeDtypeStruct((n, LANES), F32), jax.ShapeDtypeStruct((1, LANES), F32)],
        scratch_shapes=[pltpu.VMEM((1, LANES), F32)],
        compiler_params=_cparams(("arbitrary",), SMALL_VMEM_LIMIT),
        name="moe_router",
    )(x, g, rw, rb)


DMA_UNROLL = 8


def _gather_tile(idx_ref, idx_base, rows, src_ref, dst_ref, dst_base, sem, *, wait):
    def body(r, carry):
        cp = pltpu.make_async_copy(src_ref.at[idx_ref[idx_base + r]], dst_ref.at[dst_base + r], sem)
        if wait:
            cp.wait()
        else:
            cp.start()
        return carry
    lax.fori_loop(0, rows, body, 0, unroll=DMA_UNROLL)


def _cm_stride(rows):
    return rows + SUBLANES


def _rows_to_chunk_major(rows_ref, cm_ref, rows, stride):
    def body(r, carry):
        cm_ref[pl.ds(r, ROW_CHUNKS, stride=stride), :] = rows_ref[r]
        return carry
    lax.fori_loop(0, rows, body, 0, unroll=DMA_UNROLL)


def _chunk_major_to_rows(cm_ref, rows_ref, rows, stride):
    def body(r, carry):
        rows_ref[r] = cm_ref[pl.ds(r, ROW_CHUNKS, stride=stride), :]
        return carry
    lax.fori_loop(0, rows, body, 0, unroll=DMA_UNROLL)


def _expert_kernel(src_ref, te_ref, na_ref, x3_ref, g_ref, wg_ref, wu_ref, wd_ref, o_ref, buf_ref, hn_ref, cm_ref,
                   sem):
    i = pl.program_id(0)
    j = pl.program_id(1)
    nj = pl.num_programs(1)
    rows = hn_ref.shape[0]
    stride = cm_ref.shape[0] // ROW_CHUNKS
    n_active = na_ref[0]
    active = i < n_active
    slot = i % 2

    def chunk(c):
        return pl.ds(c * stride, rows)

    @pl.when((j == 0) & active)
    def _():
        @pl.when(i == 0)
        def _():
            _gather_tile(src_ref, 0, rows, x3_ref, buf_ref.at[0], 0, sem.at[0], wait=False)

        @pl.when(i + 1 < n_active)
        def _():
            _gather_tile(src_ref, (i + 1) * rows, rows, x3_ref, buf_ref.at[1 - slot], 0, sem.at[1 - slot], wait=False)

        _gather_tile(src_ref, i * rows, rows, x3_ref, buf_ref.at[slot], 0, sem.at[slot], wait=True)
        _rows_to_chunk_major(buf_ref.at[slot], cm_ref, rows, stride)

        ss = jnp.zeros((rows, 1), F32)
        for c in range(ROW_CHUNKS):
            xc = cm_ref[chunk(c), :]
            ss = ss + jnp.sum(xc * xc, axis=-1, keepdims=True)
        scale = lax.rsqrt(ss / D_MODEL + RMS_EPS)
        for c in range(ROW_CHUNKS):
            csl = slice(c * LANES, (c + 1) * LANES)
            hn_ref[:, csl] = (cm_ref[chunk(c), :] * scale * g_ref[:, csl]).astype(BF16)

    @pl.when(j == 0)
    def _():
        cm_ref[...] = jnp.zeros_like(cm_ref)

    @pl.when(active)
    def _():
        hn = hn_ref[...]
        hid = (_silu(_dot(hn, wg_ref[0])) * _dot(hn, wu_ref[0])).astype(BF16)
        out = _dot(hid, wd_ref[0])
        for c in range(ROW_CHUNKS):
            cm_ref[chunk(c), :] += out[:, c * LANES:(c + 1) * LANES]

    @pl.when(j == nj - 1)
    def _():
        _chunk_major_to_rows(cm_ref, o_ref, rows, stride)


def _experts(x3, src, g, wg, wu, wd, tile_expert, n_active, te_rows):
    s_max = src.shape[0]
    n_tiles = s_max // te_rows
    tf = 256
    nj = D_FF_EXPERT // tf

    def row(i, na):
        return jnp.minimum(i, na[0] - 1)

    def jj(i, j, na):
        return jnp.where(i < na[0], j, nj - 1)

    return pl.pallas_call(
        _expert_kernel,
        grid_spec=pltpu.PrefetchScalarGridSpec(
            num_scalar_prefetch=3,
            grid=(n_tiles, nj),
            in_specs=[
                pl.BlockSpec(memory_space=pl.ANY),
                pl.BlockSpec((1, D_MODEL), lambda i, j, sr, te, na: (0, 0)),
                pl.BlockSpec((1, D_MODEL, tf), lambda i, j, sr, te, na: (te[row(i, na)], 0, jj(i, j, na))),
                pl.BlockSpec((1, D_MODEL, tf), lambda i, j, sr, te, na: (te[row(i, na)], 0, jj(i, j, na))),
                pl.BlockSpec((1, tf, D_MODEL), lambda i, j, sr, te, na: (te[row(i, na)], jj(i, j, na), 0)),
            ],
            out_specs=pl.BlockSpec((te_rows, ROW_CHUNKS, LANES), lambda i, j, sr, te, na: (i, 0, 0)),
            scratch_shapes=[
                pltpu.VMEM((2, te_rows, ROW_CHUNKS, LANES), F32),
                pltpu.VMEM((te_rows, D_MODEL), BF16),
                pltpu.VMEM((ROW_CHUNKS * _cm_stride(te_rows), LANES), F32),
                pltpu.SemaphoreType.DMA((2,)),
            ],
        ),
        out_shape=jax.ShapeDtypeStruct((s_max, ROW_CHUNKS, LANES), F32),
        compiler_params=_cparams(("arbitrary", "arbitrary")),
        name="moe_experts",
    )(src, tile_expert, n_active, x3, g, wg, wu, wd)


def _combine_kernel(back_ref, ys3_ref, x_ref, meta_ref, g_ref, y_ref, buf_ref, cm_ref, sem, *, tile0, n_tok):
    i = pl.program_id(0)
    nt = pl.num_programs(0)
    tm = x_ref.shape[0]
    stride = cm_ref.shape[0] // ROW_CHUNKS
    slot = i % 2

    def gather(tile, slt, wait):
        for kk in range(TOP_K):
            _gather_tile(back_ref, kk * n_tok + (tile0 + tile) * tm, tm, ys3_ref, buf_ref.at[slt], kk * tm,
                         sem.at[slt], wait=wait)

    @pl.when(i == 0)
    def _():
        gather(0, 0, False)

    @pl.when(i + 1 < nt)
    def _():
        gather(i + 1, 1 - slot, False)

    gather(i, slot, True)
    _rows_to_chunk_major(buf_ref.at[slot], cm_ref, TOP_K * tm, stride)

    meta = meta_ref[...]
    p1 = meta[:, 2:3]
    p2 = meta[:, 3:4]
    ss = jnp.zeros((tm, 1), F32)
    for c in range(ROW_CHUNKS):
        csl = slice(c * LANES, (c + 1) * LANES)
        e1 = cm_ref[pl.ds(c * stride, tm), :]
        e2 = cm_ref[pl.ds(c * stride + tm, tm), :]
        xn = x_ref[:, csl] + (p1 * e1 + p2 * e2)
        ss = ss + jnp.sum(xn * xn, axis=-1, keepdims=True)
        y_ref[:, csl] = xn
    y_ref[...] = y_ref[...] * lax.rsqrt(ss / D_MODEL + RMS_EPS) * g_ref[...]


def _combine(x, ys3, back, meta, g, row0, rows):
    n = x.shape[0]
    tm = _tile(math.gcd(math.gcd(row0, rows), n), 256)
    b0 = row0 // tm
    kern = functools.partial(_combine_kernel, tile0=b0, n_tok=n)
    return pl.pallas_call(
        kern,
        grid_spec=pltpu.PrefetchScalarGridSpec(
            num_scalar_prefetch=1,
            grid=(rows // tm,),
            in_specs=[
                pl.BlockSpec(memory_space=pl.ANY),
                pl.BlockSpec((tm, D_MODEL), lambda i, bk: (b0 + i, 0)),
                pl.BlockSpec((tm, LANES), lambda i, bk: (b0 + i, 0)),
                pl.BlockSpec((1, D_MODEL), lambda i, bk: (0, 0)),
            ],
            out_specs=pl.BlockSpec((tm, D_MODEL), lambda i, bk: (i, 0)),
            scratch_shapes=[
                pltpu.VMEM((2, TOP_K * tm, ROW_CHUNKS, LANES), F32),
                pltpu.VMEM((ROW_CHUNKS * _cm_stride(TOP_K * tm), LANES), F32),
                pltpu.SemaphoreType.DMA((2,)),
            ],
        ),
        out_shape=jax.ShapeDtypeStruct((rows, D_MODEL), F32),
        compiler_params=_cparams(("arbitrary",), SMALL_VMEM_LIMIT),
        name="moe_combine",
    )(back, ys3, x, meta, g)


EXPERT_TILE = 512


def _routing_tables(meta, counts, n):
    te = EXPERT_TILE
    i1 = meta[:, 0].astype(jnp.int32)
    i2 = meta[:, 1].astype(jnp.int32)
    r1 = meta[:, 4].astype(jnp.int32)
    r2 = meta[:, 5].astype(jnp.int32)
    cnt = counts[0, :N_EXPERTS].astype(jnp.int32)
    padded = (cnt + te - 1) // te * te
    ends = jnp.cumsum(padded)
    offs = ends - padded
    slot1 = offs[i1] + r1
    slot2 = offs[i2] + r2
    s_max = (TOP_K * n + N_EXPERTS * (te - 1) + te - 1) // te * te
    tok = jnp.arange(n, dtype=jnp.int32)
    src = jnp.zeros((s_max,), jnp.int32).at[slot1].set(tok).at[slot2].set(tok)
    n_active = (ends[-1] // te).astype(jnp.int32).reshape(1)
    starts = jnp.arange(s_max // te, dtype=jnp.int32) * te
    tile_expert = jnp.minimum(jnp.searchsorted(ends, starts, side="right"), N_EXPERTS - 1).astype(jnp.int32)
    return src, jnp.concatenate([slot1, slot2]), tile_expert, n_active


def _prep_decay_params(a_log, dt_bias):
    rows = jnp.zeros((SUBLANES, LANES), F32)
    rows = rows.at[0, DN_HEADS:2 * DN_HEADS].set(a_log)
    rows = rows.at[1, DN_HEADS:2 * DN_HEADS].set(dt_bias)
    return rows


def _prep_gmlp_params(sp_w, sp_b, sample_len):
    reps = GMLP_CHUNK // sample_len
    w_s = jnp.tile(sp_w[:, :sample_len, :sample_len], (1, reps, reps))
    b_s = jnp.tile(sp_b[:, :sample_len], (1, reps))
    w2 = jnp.stack([sp_w, w_s])
    b2 = jnp.stack([sp_b.T, b_s.T])
    b2 = jnp.pad(b2, ((0, 0), (0, 0), (0, LANES - GMLP_GROUPS)))
    return w2, b2


def kernel(x_prompt, x_sample, state_conv, state_delta, norm_mix_g, w_in, conv_w, a_log, dt_bias, dn_norm_g,
           gm_ln_g, gm_ln_b, sp_w, sp_b, w_dn_out, w_gm_out, w_out, norm_ffn_g, ffn_wg, ffn_wu, ffn_wd,
           router_w, router_b, moe_wg, moe_wu, moe_wd, final_g):
    bp, tp, _ = x_prompt.shape
    bs, ts, _ = x_sample.shape
    n_p, n_s = bp * tp, bs * ts
    n = n_p + n_s
    assert tp % GMLP_CHUNK == 0 and GMLP_CHUNK % ts == 0 and n_s % GMLP_CHUNK == 0 and ts % CHUNK == 0

    x_parts = [x_prompt.reshape(n_p, D_MODEL), x_sample.reshape(n_s, D_MODEL)]
    w_in_t = jnp.swapaxes(w_in, 1, 2)
    conv_p, conv_s, delta_p, delta_s, v_s = [], [], [], [], []
    y_p = y_s = None
    for l in range(DEPTH):
        w_ba_t = jnp.pad(w_in_t[l, OFF_B:OFF_U, :], ((0, LANES - BA_W), (0, 0)))
        hn, p_ba = _mixnorm(x_parts, norm_mix_g[l].reshape(1, D_MODEL), w_ba_t)
        p_main = _inproj(hn, w_in_t, l)
        u, wq, lk, eg, tq, tk, tv = _delta_pre(p_main, p_ba, state_conv, l, conv_w[l],
                                               _prep_decay_params(a_log[l], dt_bias[l]), bp, tp, bs, ts)
        conv_new = jnp.concatenate([tq, tk, tv], axis=-1)[:, SUBLANES - (CONV_W - 1):, :]
        o, sp_out, ss_out = _delta_rec(u, wq, lk, eg, p_main, state_delta, l, dn_norm_g[l].reshape(1, DN_DV),
                                       bp, tp, bs, ts)
        w2, b2 = _prep_gmlp_params(sp_w[l], sp_b[l], ts)
        gm, vn_s = _gmlp(p_main, gm_ln_g[l].reshape(1, GMLP_W), gm_ln_b[l].reshape(1, GMLP_W), w2, b2, n_p, ts)
        y = _merge(o, gm, w_dn_out, w_gm_out, l, p_main)
        x = _outproj(y, w_out, l, x_parts)

        conv_p.append(conv_new[:bp])
        conv_s.append(conv_new[bp:])
        delta_p.append(sp_out)
        delta_s.append(ss_out)
        v_s.append(vn_s.reshape(bs, ts, GMLP_W))

        g_ffn = norm_ffn_g[l].reshape(1, D_MODEL)
        if l % 2 == 0:
            x = _ffn(x, g_ffn, ffn_wg[l // 2].astype(BF16), ffn_wu[l // 2].astype(BF16), ffn_wd[l // 2].astype(BF16))
            x_parts = [x]
        else:
            e = l // 2
            rw = jnp.pad(router_w[e], ((0, 0), (0, LANES - N_EXPERTS)))
            rb = jnp.pad(router_b[e], (0, LANES - N_EXPERTS), constant_values=-jnp.inf).reshape(1, LANES)
            meta, counts = _router(x, g_ffn, rw, rb)
            src, back, tile_expert, n_active = _routing_tables(meta, counts, n)
            ys3 = _experts(x.reshape(n, ROW_CHUNKS, LANES), src, g_ffn, moe_wg[e].astype(BF16),
                           moe_wu[e].astype(BF16), moe_wd[e].astype(BF16), tile_expert, n_active, EXPERT_TILE)
            if l == DEPTH - 1:
                fg = final_g.reshape(1, D_MODEL)
                y_p = _combine(x, ys3, back, meta, fg, 0, n_p)
                y_s = _combine(x, ys3, back, meta, fg, n_p, n_s)
    return (y_p.reshape(bp, tp, D_MODEL), y_s.reshape(bs, ts, D_MODEL), jnp.stack(conv_p), jnp.stack(delta_p),
            jnp.stack(conv_s), jnp.stack(delta_s), jnp.stack(v_s))
```

```python
import functools
import math

import jax
import jax.numpy as jnp
from jax import lax
from jax.experimental import pallas as pl
from jax.experimental.pallas import tpu as pltpu

F32 = jnp.float32
BF16 = jnp.bfloat16
HIGHEST = lax.Precision.HIGHEST

D_MODEL = 2048
DEPTH = 2
CHUNK = 64
DN_HEADS = 16
DN_DK = 128
DN_DV = 128
DN_QK_W = DN_HEADS * DN_DK
DN_V_W = DN_HEADS * DN_DV
DN_QKV_W = 2 * DN_QK_W + DN_V_W
CONV_W = 4
DN_SCALE = DN_DK ** -0.5
GMLP_CHUNK = 128
GMLP_GROUPS = 16
GMLP_GROUP_DIM = 128
GMLP_W = GMLP_GROUPS * GMLP_GROUP_DIM
OFF_Z = DN_QKV_W
OFF_B = OFF_Z + DN_V_W
OFF_A = OFF_B + DN_HEADS
OFF_U = OFF_A + DN_HEADS
OFF_V = OFF_U + GMLP_W
OFF_GA = OFF_V + GMLP_W
OFF_GB = OFF_GA + D_MODEL
IN_W = OFF_GB + D_MODEL
D_FF = 11 * D_MODEL // 4
N_EXPERTS = 8
TOP_K = 2
D_FF_EXPERT = D_FF // 2
RMS_EPS = 1e-6
LN_EPS = 1e-5
L2_EPS = 1e-6

LANES = 128
SUBLANES = 8
V7X_VMEM_LIMIT = 56 * 1024 * 1024
SMALL_VMEM_LIMIT = 32 * 1024 * 1024

MAIN_W = IN_W - 2 * DN_HEADS
COL_Z = OFF_Z
COL_U = COL_Z + DN_V_W
COL_V = COL_U + GMLP_W
COL_GA = COL_V + GMLP_W
COL_GB = COL_GA + D_MODEL

ROW_CHUNKS = D_MODEL // LANES
N_PAIRS = DN_HEADS // 2
PAIR_W = 2 * DN_DK
PREV_ROWS = 16
PAIR_GROUP = 8


def _tile(n, pref):
    t = pref
    while n % t:
        t //= 2
    return t


def _cparams(sem, vmem_bytes=V7X_VMEM_LIMIT):
    return pltpu.CompilerParams(dimension_semantics=sem, vmem_limit_bytes=vmem_bytes)


def _rms(x, g):
    ms = jnp.mean(x * x, axis=-1, keepdims=True)
    return x * lax.rsqrt(ms + RMS_EPS) * g


def _dot(a, b):
    return jnp.dot(a, b, preferred_element_type=F32)


def _dot_nt(a, b):
    return lax.dot_general(a, b, (((1,), (1,)), ((), ())), preferred_element_type=F32)


def _silu(x):
    return x * jax.nn.sigmoid(x)


def _gelu(x):
    return 0.5 * x * (1.0 + jnp.tanh(math.sqrt(2.0 / math.pi) * (x + 0.044715 * (x * x * x))))


def _softplus(x):
    return jnp.maximum(x, 0.0) + jnp.log1p(jnp.exp(-jnp.abs(x)))


BA_W = 2 * DN_HEADS


def _row_sources(parts, block_shape, index_fn):
    if len(parts) == 1:
        return [pl.BlockSpec(block_shape, index_fn)], 0
    tm = block_shape[0]
    assert len(parts) == 2 and all(p.shape[0] % tm == 0 for p in parts)
    t0 = parts[0].shape[0] // tm

    def first(*ids):
        r, c = index_fn(*ids)
        return jnp.minimum(r, t0 - 1), c

    def second(*ids):
        r, c = index_fn(*ids)
        return jnp.maximum(r - t0, 0), c

    return [pl.BlockSpec(block_shape, first), pl.BlockSpec(block_shape, second)], t0


def _pick_rows(refs, first_tiles, row_axis):
    if len(refs) == 1:
        return refs[0][...]
    return jnp.where(pl.program_id(row_axis) < first_tiles, refs[0][...], refs[1][...])


def _mixnorm_kernel(*refs, first_tiles):
    *x_refs, g_ref, wbat_ref, hn_ref, pba_ref = refs
    hn = _rms(_pick_rows(x_refs, first_tiles, 0), g_ref[...]).astype(BF16)
    hn_ref[...] = hn
    pba_ref[...] = _dot_nt(hn, wbat_ref[...].astype(BF16))


def _mixnorm(x_parts, g, w_ba_t):
    n = sum(p.shape[0] for p in x_parts)
    tm = _tile(math.gcd(*[p.shape[0] for p in x_parts]), 512)
    x_specs, first_tiles = _row_sources(x_parts, (tm, D_MODEL), lambda i: (i, 0))
    return pl.pallas_call(
        functools.partial(_mixnorm_kernel, first_tiles=first_tiles),
        grid=(n // tm,),
        in_specs=x_specs + [
            pl.BlockSpec((1, D_MODEL), lambda i: (0, 0)),
            pl.BlockSpec((LANES, D_MODEL), lambda i: (0, 0)),
        ],
        out_specs=[
            pl.BlockSpec((tm, D_MODEL), lambda i: (i, 0)),
            pl.BlockSpec((tm, LANES), lambda i: (i, 0)),
        ],
        out_shape=[jax.ShapeDtypeStruct((n, D_MODEL), BF16), jax.ShapeDtypeStruct((n, LANES), F32)],
        compiler_params=_cparams(("parallel",), SMALL_VMEM_LIMIT),
        name="mix_norm",
    )(*x_parts, g, w_ba_t)


def _inproj_kernel(hn_ref, wa_ref, wb_ref, p_ref, w16_ref, *, first_shifted):
    j = pl.program_id(0)

    @pl.when(pl.program_id(1) == 0)
    def _():
        @pl.when(j < first_shifted)
        def _():
            w16_ref[...] = wa_ref[...].T.astype(BF16)

        @pl.when(j >= first_shifted)
        def _():
            wt = jnp.concatenate([wa_ref[BA_W:, :], wb_ref[:BA_W, :]], axis=0)
            w16_ref[...] = wt.T.astype(BF16)

    p_ref[...] = _dot(hn_ref[...], w16_ref[...]).astype(p_ref.dtype)


def _inproj(hn, w_in_t, layer):
    n = hn.shape[0]
    tm = _tile(n, 1024)
    tn = 1024
    assert OFF_B % tn == 0 and OFF_U - OFF_B == BA_W and BA_W % SUBLANES == 0
    first_shifted = OFF_B // tn
    tail_blocks = tn // LANES
    kern = functools.partial(_inproj_kernel, first_shifted=first_shifted)
    return pl.pallas_call(
        kern,
        grid=(MAIN_W // tn, n // tm),
        in_specs=[
            pl.BlockSpec((tm, D_MODEL), lambda j, i: (i, 0)),
            pl.BlockSpec((None, tn, D_MODEL), lambda j, i: (layer, j, 0)),
            pl.BlockSpec((None, LANES, D_MODEL), lambda j, i: (layer, (j + 1) * tail_blocks, 0)),
        ],
        out_specs=pl.BlockSpec((tm, tn), lambda j, i: (i, j)),
        out_shape=jax.ShapeDtypeStruct((n, MAIN_W), BF16),
        scratch_shapes=[pltpu.VMEM((D_MODEL, tn), BF16)],
        compiler_params=_cparams(("parallel", "arbitrary")),
        name="in_proj",
    )(hn, w_in_t, w_in_t)


def _dpre_kernel(q_ref, k_ref, v_ref, pq_ref, pk_ref, pv_ref, cq_ref, ck_ref, cv_ref, cwq_ref, cwk_ref, cwv_ref,
                 ba_ref, ab_ref, u_ref, wqo_ref, lk_ref, eg_ref, tq_ref, tk_ref, tv_ref, xbuf_ref, cbuf_ref, cnext_ref,
                 *, nb, npb, ncp, ncs):
    t = pl.program_id(0)
    s = jnp.minimum(t, nb - 1)
    is_p = s < npb
    first = jnp.where(is_p, s % ncp, (s - npb) % ncs) == 0
    hist = SUBLANES - (CONV_W - 1)

    @pl.when(t == 0)
    def _():
        cbuf_ref[...] = jnp.zeros_like(cbuf_ref)

    def conv_silu(j, raw_ref, prev_ref, cp_ref, cw_ref, tail_ref):
        raw = raw_ref[...].astype(F32)
        tail_ref[0] = raw[CHUNK - SUBLANES:, :]
        tail = prev_ref[prev_ref.shape[0] - SUBLANES:, :].astype(F32)
        xbuf_ref[j] = tail
        xbuf_ref[j, hist:SUBLANES, :] = jnp.where(first, jnp.where(is_p, 0.0, cp_ref[0]), tail[hist:, :])
        rows = jnp.concatenate([xbuf_ref[j], raw], axis=0)
        acc = raw * cw_ref[CONV_W - 1:CONV_W, :]
        for sft in range(1, CONV_W):
            acc = acc + pltpu.roll(rows, sft, axis=0)[SUBLANES:] * cw_ref[CONV_W - 1 - sft:CONV_W - sft, :]
        return _silu(acc)

    ba = ba_ref[...]
    lane = lax.broadcasted_iota(jnp.int32, (CHUNK, LANES), 1)
    gval = -jnp.exp(ab_ref[0:1, :]) * _softplus(ba + ab_ref[1:2, :])
    bg = jnp.where(lane < DN_HEADS, jax.nn.sigmoid(ba), gval)
    r64 = lax.broadcasted_iota(jnp.int32, (CHUNK, CHUNK), 0)
    c64 = lax.broadcasted_iota(jnp.int32, (CHUNK, CHUNK), 1)
    gc_cols = jnp.dot((r64 >= c64).astype(F32), bg, precision=HIGHEST, preferred_element_type=F32)
    gc_rows = gc_cols.T
    gc_rows2 = jnp.concatenate([gc_rows, gc_rows], axis=1)

    ri = lax.broadcasted_iota(jnp.int32, (2 * CHUNK, 2 * CHUNK), 0)
    ci = lax.broadcasted_iota(jnp.int32, (2 * CHUNK, 2 * CHUNK), 1)
    same = (ri // CHUNK) == (ci // CHUNK)
    causal = same & (ri >= ci)
    strict = same & (ri > ci)
    left = lax.broadcasted_iota(jnp.int32, (1, 2 * CHUNK), 1) < CHUNK

    def stack(j, p):
        return jnp.concatenate([cbuf_ref[j, :, (2 * p) * DN_DK:(2 * p + 1) * DN_DK],
                                cbuf_ref[j, :, (2 * p + 1) * DN_DK:(2 * p + 2) * DN_DK]], axis=0)

    def colpair(a, off, p):
        return jnp.concatenate([a[:, off + 2 * p:off + 2 * p + 1], a[:, off + 2 * p + 1:off + 2 * p + 2]], axis=0)

    def pair_group(pairs):
        beta = {p: colpair(bg, 0, p) for p in pairs}
        gcc = {p: colpair(gc_cols, DN_HEADS, p) for p in pairs}
        gcr = {p: jnp.where(left, gc_rows2[DN_HEADS + 2 * p:DN_HEADS + 2 * p + 1, :],
                            gc_rows2[DN_HEADS + 2 * p + 1:DN_HEADS + 2 * p + 2, :]) for p in pairs}
        gl = {p: [gc_cols[CHUNK - 1:CHUNK, DN_HEADS + 2 * p + t:DN_HEADS + 2 * p + t + 1] for t in (0, 1)]
              for p in pairs}
        glr = {p: jnp.where(left, gl[p][0], gl[p][1]) for p in pairs}
        decay = {p: jnp.exp(jnp.where(causal, gcc[p] - gcr[p], -jnp.inf)) for p in pairs}
        egc = {p: jnp.exp(gcc[p]) for p in pairs}

        qh = {p: stack(0, p) for p in pairs}
        kh = {p: stack(1, p) for p in pairs}
        q = {p: qh[p] * (lax.rsqrt(jnp.sum(qh[p] * qh[p], axis=-1, keepdims=True) + L2_EPS) * DN_SCALE) for p in pairs}
        k = {p: kh[p] * lax.rsqrt(jnp.sum(kh[p] * kh[p], axis=-1, keepdims=True) + L2_EPS) for p in pairs}
        kb = {p: k[p] * beta[p] for p in pairs}
        k16 = {p: k[p].astype(BF16) for p in pairs}

        m = {p: jnp.where(strict, -(_dot_nt(kb[p].astype(BF16), k16[p]) * decay[p]), 0.0) for p in pairs}
        r = dict(m)
        for _ in range(5):
            m16 = {p: m[p].astype(BF16) for p in pairs}
            m = {p: _dot(m16[p], m16[p]) for p in pairs}
            r = {p: r[p] + m[p] + _dot(r[p].astype(BF16), m[p].astype(BF16)) for p in pairs}
        rhs = {p: jnp.concatenate([stack(2, p) * beta[p], kb[p] * egc[p]], axis=1) for p in pairs}
        uw = {p: rhs[p] + _dot(r[p].astype(BF16), rhs[p].astype(BF16)) for p in pairs}
        qk = {p: _dot_nt(q[p].astype(BF16), k16[p]) * decay[p] for p in pairs}
        kdt = {p: k[p].T * jnp.exp(glr[p] - gcr[p]) for p in pairs}

        for p in pairs:
            for t in (0, 1):
                sl = slice((2 * p + t) * DN_DK, (2 * p + t + 1) * DN_DK)
                rows = slice(t * CHUNK, (t + 1) * CHUNK)
                u_ref[:, sl] = uw[p][rows, :DN_DV]
                wqo_ref[0:CHUNK, sl] = uw[p][rows, DN_DV:].astype(BF16)
                wqo_ref[CHUNK:2 * CHUNK, sl] = (q[p][rows] * egc[p][rows]).astype(BF16)
                eg_ref[:, sl] = jnp.broadcast_to(jnp.exp(gl[p][t]), (SUBLANES, DN_DV))
            psl = slice(p * 2 * CHUNK, (p + 1) * 2 * CHUNK)
            lk_ref[0:CHUNK, psl] = jnp.where(left, qk[p][:CHUNK], qk[p][CHUNK:]).astype(BF16)
            lk_ref[CHUNK:3 * CHUNK, psl] = kdt[p].astype(BF16)

    for first_pair in range(0, N_PAIRS, PAIR_GROUP):
        pair_group(range(first_pair, first_pair + PAIR_GROUP))

    cnext_ref[0] = conv_silu(0, q_ref, pq_ref, cq_ref, cwq_ref, tq_ref)
    cnext_ref[1] = conv_silu(1, k_ref, pk_ref, ck_ref, cwk_ref, tk_ref)
    cnext_ref[2] = conv_silu(2, v_ref, pv_ref, cv_ref, cwv_ref, tv_ref)
    cbuf_ref[...] = cnext_ref[...]


def _delta_pre(p_main, p_ba, state_conv, layer, conv_w, ab, bp, tp, bs, ts):
    n = p_main.shape[0]
    ncp, ncs = tp // CHUNK, ts // CHUNK
    npb = bp * ncp
    nb = n // CHUNK
    rows_per_blk = CHUNK // PREV_ROWS

    def conv_blk(t):
        return jnp.minimum(t, nb - 1)

    def mat_blk(t):
        return jnp.maximum(t - 1, 0)

    def seq_s(t):
        return jnp.maximum(conv_blk(t) - npb, 0) // ncs

    def seq_all(t):
        blk = conv_blk(t)
        return jnp.where(blk < npb, blk // ncp, bp + (blk - npb) // ncs)

    def col(c):
        return pl.BlockSpec((CHUNK, DN_QK_W), lambda t, c=c: (conv_blk(t), c))

    def prev(c):
        return pl.BlockSpec((PREV_ROWS, DN_QK_W),
                            lambda t, c=c: (jnp.maximum(conv_blk(t) * rows_per_blk - 1, 0), c))

    def cprev(c):
        return pl.BlockSpec((None, 1, CONV_W - 1, DN_QK_W), lambda t, c=c: (layer, seq_s(t), 0, c))

    def cw(c):
        return pl.BlockSpec((CONV_W, DN_QK_W), lambda t, c=c: (0, c))

    kern = functools.partial(_dpre_kernel, nb=nb, npb=npb, ncp=ncp, ncs=ncs)
    return pl.pallas_call(
        kern,
        grid=(nb + 1,),
        in_specs=[
            col(0), col(1), col(2), prev(0), prev(1), prev(2), cprev(0), cprev(1), cprev(2), cw(0), cw(1), cw(2),
            pl.BlockSpec((CHUNK, LANES), lambda t: (mat_blk(t), 0)),
            pl.BlockSpec((SUBLANES, LANES), lambda t: (0, 0)),
        ],
        out_specs=[
            pl.BlockSpec((CHUNK, DN_V_W), lambda t: (mat_blk(t), 0)),
            pl.BlockSpec((2 * CHUNK, DN_QK_W), lambda t: (mat_blk(t), 0)),
            pl.BlockSpec((3 * CHUNK, DN_HEADS * CHUNK), lambda t: (mat_blk(t), 0)),
            pl.BlockSpec((SUBLANES, DN_V_W), lambda t: (mat_blk(t), 0)),
        ] + [pl.BlockSpec((1, SUBLANES, DN_QK_W), lambda t: (seq_all(t), 0, 0))] * 3,
        out_shape=[
            jax.ShapeDtypeStruct((n, DN_V_W), F32),
            jax.ShapeDtypeStruct((2 * n, DN_QK_W), BF16),
            jax.ShapeDtypeStruct((3 * n, DN_HEADS * CHUNK), BF16),
            jax.ShapeDtypeStruct((nb * SUBLANES, DN_V_W), F32),
        ] + [jax.ShapeDtypeStruct((bp + bs, SUBLANES, DN_QK_W), F32)] * 3,
        scratch_shapes=[pltpu.VMEM((3, SUBLANES, DN_QK_W), F32), pltpu.VMEM((3, CHUNK, DN_QK_W), F32),
                        pltpu.VMEM((3, CHUNK, DN_QK_W), F32)],
        compiler_params=_cparams(("arbitrary",), SMALL_VMEM_LIMIT),
        name="delta_pre",
    )(p_main, p_main, p_main, p_main, p_main, p_main, state_conv, state_conv, state_conv, conv_w, conv_w, conv_w,
      p_ba, ab)


def _drec_kernel(u_ref, wq_ref, lk_ref, eg_ref, z_ref, sp_ref, dng_ref, o_ref, sop_ref, sos_ref, st_ref,
                 *, npb, ncp, ncs):
    s = pl.program_id(0)
    is_p = s < npb
    cidx = jnp.where(is_p, s % ncp, (s - npb) % ncs)
    first = cidx == 0
    last = cidx == jnp.where(is_p, ncp, ncs) - 1

    @pl.when(first)
    def _():
        for h in range(DN_HEADS):
            st_ref[:, h * DN_DV:(h + 1) * DN_DV] = jnp.where(is_p, 0.0, sp_ref[0, h])

    lane = lax.broadcasted_iota(jnp.int32, (1, PAIR_W), 1)
    left = lane < DN_DV
    dng = dng_ref[...]
    for p in range(N_PAIRS):
        psl = slice(p * PAIR_W, (p + 1) * PAIR_W)
        st = st_ref[:, psl]
        st16 = st.astype(BF16)
        zero = jnp.zeros_like(st16)
        sbd = jnp.concatenate([jnp.where(left, st16, zero), jnp.where(left, zero, st16)], axis=0)
        ws = _dot(wq_ref[:, psl], sbd)
        v_new = (u_ref[:, psl] - ws[:CHUNK]).astype(BF16)
        vzero = jnp.zeros_like(v_new)
        vbd = jnp.concatenate([jnp.where(left, v_new, vzero), jnp.where(left, vzero, v_new)], axis=0)
        t = _dot(lk_ref[:, p * 2 * CHUNK:(p + 1) * 2 * CHUNK], vbd)
        o = ws[CHUNK:] + t[:CHUNK]
        st_ref[:, psl] = st * eg_ref[0:1, psl] + t[CHUNK:]
        for hh in (0, 1):
            sl = slice(p * PAIR_W + hh * DN_DV, p * PAIR_W + (hh + 1) * DN_DV)
            oh = o[:, hh * DN_DV:(hh + 1) * DN_DV]
            o_ref[:, sl] = (_rms(oh, dng) * _silu(z_ref[:, sl].astype(F32))).astype(o_ref.dtype)

    @pl.when(last & is_p)
    def _():
        for h in range(DN_HEADS):
            sop_ref[0, h] = st_ref[:, h * DN_DV:(h + 1) * DN_DV]

    @pl.when(last & jnp.logical_not(is_p))
    def _():
        for h in range(DN_HEADS):
            sos_ref[0, h] = st_ref[:, h * DN_DV:(h + 1) * DN_DV]


def _delta_rec(u, wq, lk, eg, p_main, state_delta, layer, dn_g, bp, tp, bs, ts):
    n = u.shape[0]
    ncp, ncs = tp // CHUNK, ts // CHUNK
    npb = bp * ncp
    nb = n // CHUNK
    zblk = COL_Z // DN_V_W

    def seq_s(s):
        return jnp.maximum(s - npb, 0) // ncs

    def seq_p(s):
        return jnp.minimum(s // ncp, bp - 1)

    kern = functools.partial(_drec_kernel, npb=npb, ncp=ncp, ncs=ncs)
    return pl.pallas_call(
        kern,
        grid=(nb,),
        in_specs=[
            pl.BlockSpec((CHUNK, DN_V_W), lambda s: (s, 0)),
            pl.BlockSpec((2 * CHUNK, DN_QK_W), lambda s: (s, 0)),
            pl.BlockSpec((3 * CHUNK, DN_HEADS * CHUNK), lambda s: (s, 0)),
            pl.BlockSpec((SUBLANES, DN_V_W), lambda s: (s, 0)),
            pl.BlockSpec((CHUNK, DN_V_W), lambda s: (s, zblk)),
            pl.BlockSpec((None, 1, DN_HEADS, DN_DK, DN_DV), lambda s: (layer, seq_s(s), 0, 0, 0)),
            pl.BlockSpec((1, DN_DV), lambda s: (0, 0)),
        ],
        out_specs=[
            pl.BlockSpec((CHUNK, DN_V_W), lambda s: (s, 0)),
            pl.BlockSpec((1, DN_HEADS, DN_DK, DN_DV), lambda s: (seq_p(s), 0, 0, 0)),
            pl.BlockSpec((1, DN_HEADS, DN_DK, DN_DV), lambda s: (seq_s(s), 0, 0, 0)),
        ],
        out_shape=[
            jax.ShapeDtypeStruct((n, DN_V_W), BF16),
            jax.ShapeDtypeStruct((bp, DN_HEADS, DN_DK, DN_DV), F32),
            jax.ShapeDtypeStruct((bs, DN_HEADS, DN_DK, DN_DV), F32),
        ],
        scratch_shapes=[pltpu.VMEM((DN_DK, DN_V_W), F32)],
        compiler_params=_cparams(("arbitrary",), SMALL_VMEM_LIMIT),
        name="delta_rec",
    )(u, wq, lk, eg, p_main, state_delta, dn_g)


def _gmlp_kernel(u_ref, v_ref, lg_ref, lb_ref, w_ref, b_ref, gm_ref, vn_ref, *, npb, sample_len):
    s = pl.program_id(0)
    is_s = s >= npb
    gu = _gelu(u_ref[...].astype(F32))
    gv = _gelu(v_ref[...].astype(F32))
    xc = gv - jnp.mean(gv, axis=-1, keepdims=True)
    var = jnp.mean(xc * xc, axis=-1, keepdims=True)
    vn = xc * lax.rsqrt(var + LN_EPS) * lg_ref[...] + lb_ref[...]

    @pl.when(is_s)
    def _():
        vn_ref[...] = vn

    ri = lax.broadcasted_iota(jnp.int32, (GMLP_CHUNK, GMLP_CHUNK), 0)
    ci = lax.broadcasted_iota(jnp.int32, (GMLP_CHUNK, GMLP_CHUNK), 1)
    same_seq = (ri // sample_len) == (ci // sample_len)
    mask = (ri >= ci) & (same_seq | jnp.logical_not(is_s))
    bias = b_ref[0]
    for g in range(GMLP_GROUPS):
        sl = slice(g * GMLP_GROUP_DIM, (g + 1) * GMLP_GROUP_DIM)
        wl = jnp.where(mask, w_ref[0, g], 0.0).astype(BF16)
        mixed = _dot(wl, vn[:, sl].astype(BF16)) + bias[:, g:g + 1]
        gm_ref[:, sl] = (gu[:, sl] * mixed).astype(gm_ref.dtype)


def _gmlp(p_main, ln_g, ln_b, w2, b2, n_prompt, sample_len):
    n = p_main.shape[0]
    nb = n // GMLP_CHUNK
    npb = n_prompt // GMLP_CHUNK
    ublk = COL_U // GMLP_W
    vblk = COL_V // GMLP_W
    kern = functools.partial(_gmlp_kernel, npb=npb, sample_len=sample_len)
    return pl.pallas_call(
        kern,
        grid=(nb,),
        in_specs=[
            pl.BlockSpec((GMLP_CHUNK, GMLP_W), lambda s: (s, ublk)),
            pl.BlockSpec((GMLP_CHUNK, GMLP_W), lambda s: (s, vblk)),
            pl.BlockSpec((1, GMLP_W), lambda s: (0, 0)),
            pl.BlockSpec((1, GMLP_W), lambda s: (0, 0)),
            pl.BlockSpec((1, GMLP_GROUPS, GMLP_CHUNK, GMLP_CHUNK), lambda s: (jnp.where(s >= npb, 1, 0), 0, 0, 0)),
            pl.BlockSpec((1, GMLP_CHUNK, LANES), lambda s: (jnp.where(s >= npb, 1, 0), 0, 0)),
        ],
        out_specs=[
            pl.BlockSpec((GMLP_CHUNK, GMLP_W), lambda s: (s, 0)),
            pl.BlockSpec((GMLP_CHUNK, GMLP_W), lambda s: (jnp.maximum(s - npb, 0), 0)),
        ],
        out_shape=[
            jax.ShapeDtypeStruct((n, GMLP_W), BF16),
            jax.ShapeDtypeStruct((n - n_prompt, GMLP_W), F32),
        ],
        compiler_params=_cparams(("arbitrary",), SMALL_VMEM_LIMIT),
        name="gmlp",
    )(p_main, p_main, ln_g, ln_b, w2, b2)


def _merge_kernel(o_ref, gm_ref, wd_ref, wg_ref, ga_ref, gb_ref, y_ref, wd16_ref, wg16_ref):
    @pl.when(pl.program_id(1) == 0)
    def _():
        wd16_ref[...] = wd_ref[...].astype(BF16)
        wg16_ref[...] = wg_ref[...].astype(BF16)

    a = _dot(o_ref[...], wd16_ref[...])
    b = _dot(gm_ref[...], wg16_ref[...])
    ga = jax.nn.sigmoid(ga_ref[...].astype(F32))
    gb = jax.nn.sigmoid(gb_ref[...].astype(F32))
    y_ref[...] = (ga * a + gb * b).astype(y_ref.dtype)


def _merge(o, gm, w_dn, w_gm, layer, p_main):
    n = o.shape[0]
    tm = _tile(n, 256)
    tn = 1024
    ga0, gb0 = COL_GA // tn, COL_GB // tn
    return pl.pallas_call(
        _merge_kernel,
        grid=(D_MODEL // tn, n // tm),
        in_specs=[
            pl.BlockSpec((tm, DN_V_W), lambda j, i: (i, 0)),
            pl.BlockSpec((tm, GMLP_W), lambda j, i: (i, 0)),
            pl.BlockSpec((None, DN_V_W, tn), lambda j, i: (layer, 0, j)),
            pl.BlockSpec((None, GMLP_W, tn), lambda j, i: (layer, 0, j)),
            pl.BlockSpec((tm, tn), lambda j, i: (i, ga0 + j)),
            pl.BlockSpec((tm, tn), lambda j, i: (i, gb0 + j)),
        ],
        out_specs=pl.BlockSpec((tm, tn), lambda j, i: (i, j)),
        out_shape=jax.ShapeDtypeStruct((n, D_MODEL), BF16),
        scratch_shapes=[pltpu.VMEM((DN_V_W, tn), BF16), pltpu.VMEM((GMLP_W, tn), BF16)],
        compiler_params=_cparams(("parallel", "arbitrary")),
        name="merge",
    )(o, gm, w_dn, w_gm, p_main, p_main)


def _outproj_kernel(y_ref, w_ref, *refs, first_tiles):
    *x_refs, o_ref, w16_ref = refs

    @pl.when(pl.program_id(1) == 0)
    def _():
        w16_ref[...] = w_ref[...].astype(BF16)

    o_ref[...] = _pick_rows(x_refs, first_tiles, 1) + _dot(y_ref[...], w16_ref[...])


def _outproj(y, w, layer, x_parts):
    n = y.shape[0]
    tm = _tile(math.gcd(*[p.shape[0] for p in x_parts]), 512)
    tn = 1024
    x_specs, first_tiles = _row_sources(x_parts, (tm, tn), lambda j, i: (i, j))
    return pl.pallas_call(
        functools.partial(_outproj_kernel, first_tiles=first_tiles),
        grid=(D_MODEL // tn, n // tm),
        in_specs=[
            pl.BlockSpec((tm, D_MODEL), lambda j, i: (i, 0)),
            pl.BlockSpec((None, D_MODEL, tn), lambda j, i: (layer, 0, j)),
        ] + x_specs,
        out_specs=pl.BlockSpec((tm, tn), lambda j, i: (i, j)),
        out_shape=jax.ShapeDtypeStruct((n, D_MODEL), F32),
        scratch_shapes=[pltpu.VMEM((D_MODEL, tn), BF16)],
        compiler_params=_cparams(("parallel", "arbitrary")),
        name="out_proj",
    )(y, w, *x_parts)


def _ffn_kernel(x_ref, g_ref, wg_ref, wu_ref, wd_ref, o_ref, hn_ref):
    @pl.when(pl.program_id(1) == 0)
    def _():
        x = x_ref[...]
        hn_ref[...] = _rms(x, g_ref[...]).astype(BF16)
        o_ref[...] = x

    hn = hn_ref[...]
    hid = (_silu(_dot(hn, wg_ref[...])) * _dot(hn, wu_ref[...])).astype(BF16)
    o_ref[...] += _dot(hid, wd_ref[...])


def _ffn(x, g, wg, wu, wd):
    n = x.shape[0]
    tm = _tile(n, 1024)
    tf = 512
    return pl.pallas_call(
        _ffn_kernel,
        grid=(n // tm, D_FF // tf),
        in_specs=[
            pl.BlockSpec((tm, D_MODEL), lambda i, j: (i, 0)),
            pl.BlockSpec((1, D_MODEL), lambda i, j: (0, 0)),
            pl.BlockSpec((D_MODEL, tf), lambda i, j: (0, j)),
            pl.BlockSpec((D_MODEL, tf), lambda i, j: (0, j)),
            pl.BlockSpec((tf, D_MODEL), lambda i, j: (j, 0)),
        ],
        out_specs=pl.BlockSpec((tm, D_MODEL), lambda i, j: (i, 0)),
        out_shape=jax.ShapeDtypeStruct((n, D_MODEL), F32),
        scratch_shapes=[pltpu.VMEM((tm, D_MODEL), BF16)],
        compiler_params=_cparams(("parallel", "arbitrary")),
        name="ffn_dense",
    )(x, g, wg, wu, wd)


def _router_kernel(x_ref, g_ref, rw_ref, rb_ref, meta_ref, cnt_ref, carry_ref):
    @pl.when(pl.program_id(0) == 0)
    def _():
        carry_ref[...] = jnp.zeros_like(carry_ref)

    tm = x_ref.shape[0]
    hn = _rms(x_ref[...], g_ref[...])
    logits = jnp.dot(hn, rw_ref[...], precision=HIGHEST, preferred_element_type=F32) + rb_ref[...]
    lane = lax.broadcasted_iota(jnp.int32, (tm, LANES), 1).astype(F32)
    m1 = jnp.max(logits, axis=-1, keepdims=True)
    i1 = jnp.min(jnp.where(logits == m1, lane, float(LANES)), axis=-1, keepdims=True)
    oh1 = lane == i1
    rest = jnp.where(oh1, -jnp.inf, logits)
    m2 = jnp.max(rest, axis=-1, keepdims=True)
    i2 = jnp.min(jnp.where(rest == m2, lane, float(LANES)), axis=-1, keepdims=True)
    oh2 = lane == i2
    e = jnp.exp(m2 - m1)
    p1 = 1.0 / (1.0 + e)
    p2 = e / (1.0 + e)

    onehot = jnp.where(oh1 | oh2, 1.0, 0.0)
    ri = lax.broadcasted_iota(jnp.int32, (tm, tm), 0)
    ci = lax.broadcasted_iota(jnp.int32, (tm, tm), 1)
    before = jnp.where(ri > ci, 1.0, 0.0).astype(BF16)
    rank = _dot(before, onehot.astype(BF16)) + carry_ref[...]
    r1 = jnp.sum(jnp.where(oh1, rank, 0.0), axis=-1, keepdims=True)
    r2 = jnp.sum(jnp.where(oh2, rank, 0.0), axis=-1, keepdims=True)
    carry_ref[...] += jnp.sum(onehot, axis=0, keepdims=True)

    meta = jnp.zeros((tm, LANES), F32)
    for idx, val in enumerate((i1, i2, p1, p2, r1, r2)):
        meta = jnp.where(lane == float(idx), val, meta)
    meta_ref[...] = meta
    cnt_ref[...] = carry_ref[...]


def _router(x, g, rw, rb):
    n = x.shape[0]
    tm = _tile(n, 256)
    return pl.pallas_call(
        _router_kernel,
        grid=(n // tm,),
        in_specs=[
            pl.BlockSpec((tm, D_MODEL), lambda i: (i, 0)),
            pl.BlockSpec((1, D_MODEL), lambda i: (0, 0)),
            pl.BlockSpec((D_MODEL, LANES), lambda i: (0, 0)),
            pl.BlockSpec((1, LANES), lambda i: (0, 0)),
        ],
        out_specs=[
            pl.BlockSpec((tm, LANES), lambda i: (i, 0)),
            pl.BlockSpec((1, LANES), lambda i: (0, 0)),
        ],
        out_shape=[jax.ShapeDtypeStruct((n, LANES), F32), jax.ShapeDtypeStruct((1, LANES), F32)],
        scratch_shapes=[pltpu.VMEM((1, LANES), F32)],
        compiler_params=_cparams(("arbitrary",), SMALL_VMEM_LIMIT),
        name="moe_router",
    )(x, g, rw, rb)


DMA_UNROLL = 32


def _gather_tile(idx_ref, idx_base, rows, src_ref, dst_ref, dst_base, sem, *, wait):
    def body(r, carry):
        cp = pltpu.make_async_copy(src_ref.at[idx_ref[idx_base + r]], dst_ref.at[dst_base + r], sem)
        if wait:
            cp.wait()
        else:
            cp.start()
        return carry
    lax.fori_loop(0, rows, body, 0, unroll=DMA_UNROLL)


def _cm_stride(rows):
    return rows + SUBLANES


def _rows_to_chunk_major(rows_ref, cm_ref, rows, stride):
    def body(r, carry):
        cm_ref[pl.ds(r, ROW_CHUNKS, stride=stride), :] = rows_ref[r]
        return carry
    lax.fori_loop(0, rows, body, 0, unroll=DMA_UNROLL)


def _chunk_major_to_rows(cm_ref, rows_ref, rows, stride):
    def body(r, carry):
        rows_ref[r] = cm_ref[pl.ds(r, ROW_CHUNKS, stride=stride), :]
        return carry
    lax.fori_loop(0, rows, body, 0, unroll=DMA_UNROLL)


def _expert_kernel(src_ref, te_ref, na_ref, x3_ref, g_ref, wg_ref, wu_ref, wd_ref, o_ref, buf_ref, hn_ref, cm_ref,
                   sem):
    i = pl.program_id(0)
    j = pl.program_id(1)
    nj = pl.num_programs(1)
    rows = hn_ref.shape[0]
    stride = cm_ref.shape[0] // ROW_CHUNKS
    n_active = na_ref[0]
    active = i < n_active
    slot = i % 2

    def chunk(c):
        return pl.ds(c * stride, rows)

    @pl.when((j == 0) & active)
    def _():
        @pl.when(i == 0)
        def _():
            _gather_tile(src_ref, 0, rows, x3_ref, buf_ref.at[0], 0, sem.at[0], wait=False)

        @pl.when(i + 1 < n_active)
        def _():
            _gather_tile(src_ref, (i + 1) * rows, rows, x3_ref, buf_ref.at[1 - slot], 0, sem.at[1 - slot], wait=False)

        _gather_tile(src_ref, i * rows, rows, x3_ref, buf_ref.at[slot], 0, sem.at[slot], wait=True)
        _rows_to_chunk_major(buf_ref.at[slot], cm_ref, rows, stride)

        ss = jnp.zeros((rows, 1), F32)
        for c in range(ROW_CHUNKS):
            xc = cm_ref[chunk(c), :]
            ss = ss + jnp.sum(xc * xc, axis=-1, keepdims=True)
        scale = lax.rsqrt(ss / D_MODEL + RMS_EPS)
        for c in range(ROW_CHUNKS):
            csl = slice(c * LANES, (c + 1) * LANES)
            hn_ref[:, csl] = (cm_ref[chunk(c), :] * scale * g_ref[:, csl]).astype(BF16)

    @pl.when(j == 0)
    def _():
        cm_ref[...] = jnp.zeros_like(cm_ref)

    @pl.when(active)
    def _():
        hn = hn_ref[...]
        hid = (_silu(_dot(hn, wg_ref[0])) * _dot(hn, wu_ref[0])).astype(BF16)
        out = _dot(hid, wd_ref[0])
        for c in range(ROW_CHUNKS):
            cm_ref[chunk(c), :] += out[:, c * LANES:(c + 1) * LANES]

    @pl.when(j == nj - 1)
    def _():
        _chunk_major_to_rows(cm_ref, o_ref, rows, stride)


def _experts(x3, src, g, wg, wu, wd, tile_expert, n_active, te_rows):
    s_max = src.shape[0]
    n_tiles = s_max // te_rows
    tf = 256
    nj = D_FF_EXPERT // tf

    def row(i, na):
        return jnp.minimum(i, na[0] - 1)

    def jj(i, j, na):
        return jnp.where(i < na[0], j, nj - 1)

    return pl.pallas_call(
        _expert_kernel,
        grid_spec=pltpu.PrefetchScalarGridSpec(
            num_scalar_prefetch=3,
            grid=(n_tiles, nj),
            in_specs=[
                pl.BlockSpec(memory_space=pl.ANY),
                pl.BlockSpec((1, D_MODEL), lambda i, j, sr, te, na: (0, 0)),
                pl.BlockSpec((1, D_MODEL, tf), lambda i, j, sr, te, na: (te[row(i, na)], 0, jj(i, j, na))),
                pl.BlockSpec((1, D_MODEL, tf), lambda i, j, sr, te, na: (te[row(i, na)], 0, jj(i, j, na))),
                pl.BlockSpec((1, tf, D_MODEL), lambda i, j, sr, te, na: (te[row(i, na)], jj(i, j, na), 0)),
            ],
            out_specs=pl.BlockSpec((te_rows, ROW_CHUNKS, LANES), lambda i, j, sr, te, na: (i, 0, 0)),
            scratch_shapes=[
                pltpu.VMEM((2, te_rows, ROW_CHUNKS, LANES), F32),
                pltpu.VMEM((te_rows, D_MODEL), BF16),
                pltpu.VMEM((ROW_CHUNKS * _cm_stride(te_rows), LANES), F32),
                pltpu.SemaphoreType.DMA((2,)),
            ],
        ),
        out_shape=jax.ShapeDtypeStruct((s_max, ROW_CHUNKS, LANES), F32),
        compiler_params=_cparams(("arbitrary", "arbitrary")),
        name="moe_experts",
    )(src, tile_expert, n_active, x3, g, wg, wu, wd)


def _combine_kernel(back_ref, ys3_ref, x_ref, meta_ref, g_ref, y_ref, buf_ref, cm_ref, sem, *, tile0, n_tok):
    i = pl.program_id(0)
    nt = pl.num_programs(0)
    tm = x_ref.shape[0]
    stride = cm_ref.shape[0] // ROW_CHUNKS
    slot = i % 2

    def gather(tile, slt, wait):
        for kk in range(TOP_K):
            _gather_tile(back_ref, kk * n_tok + (tile0 + tile) * tm, tm, ys3_ref, buf_ref.at[slt], kk * tm,
                         sem.at[slt], wait=wait)

    @pl.when(i == 0)
    def _():
        gather(0, 0, False)

    @pl.when(i + 1 < nt)
    def _():
        gather(i + 1, 1 - slot, False)

    gather(i, slot, True)
    _rows_to_chunk_major(buf_ref.at[slot], cm_ref, TOP_K * tm, stride)

    meta = meta_ref[...]
    p1 = meta[:, 2:3]
    p2 = meta[:, 3:4]
    ss = jnp.zeros((tm, 1), F32)
    for c in range(ROW_CHUNKS):
        csl = slice(c * LANES, (c + 1) * LANES)
        e1 = cm_ref[pl.ds(c * stride, tm), :]
        e2 = cm_ref[pl.ds(c * stride + tm, tm), :]
        xn = x_ref[:, csl] + (p1 * e1 + p2 * e2)
        ss = ss + jnp.sum(xn * xn, axis=-1, keepdims=True)
        y_ref[:, csl] = xn
    y_ref[...] = y_ref[...] * lax.rsqrt(ss / D_MODEL + RMS_EPS) * g_ref[...]


def _combine(x, ys3, back, meta, g, row0, rows):
    n = x.shape[0]
    tm = _tile(math.gcd(math.gcd(row0, rows), n), 256)
    b0 = row0 // tm
    kern = functools.partial(_combine_kernel, tile0=b0, n_tok=n)
    return pl.pallas_call(
        kern,
        grid_spec=pltpu.PrefetchScalarGridSpec(
            num_scalar_prefetch=1,
            grid=(rows // tm,),
            in_specs=[
                pl.BlockSpec(memory_space=pl.ANY),
                pl.BlockSpec((tm, D_MODEL), lambda i, bk: (b0 + i, 0)),
                pl.BlockSpec((tm, LANES), lambda i, bk: (b0 + i, 0)),
                pl.BlockSpec((1, D_MODEL), lambda i, bk: (0, 0)),
            ],
            out_specs=pl.BlockSpec((tm, D_MODEL), lambda i, bk: (i, 0)),
            scratch_shapes=[
                pltpu.VMEM((2, TOP_K * tm, ROW_CHUNKS, LANES), F32),
                pltpu.VMEM((ROW_CHUNKS * _cm_stride(TOP_K * tm), LANES), F32),
                pltpu.SemaphoreType.DMA((2,)),
            ],
        ),
        out_shape=jax.ShapeDtypeStruct((rows, D_MODEL), F32),
        compiler_params=_cparams(("arbitrary",), SMALL_VMEM_LIMIT),
        name="moe_combine",
    )(back, ys3, x, meta, g)


EXPERT_TILE = 512


def _routing_tables(meta, counts, n):
    te = EXPERT_TILE
    i1 = meta[:, 0].astype(jnp.int32)
    i2 = meta[:, 1].astype(jnp.int32)
    r1 = meta[:, 4].astype(jnp.int32)
    r2 = meta[:, 5].astype(jnp.int32)
    cnt = counts[0, :N_EXPERTS].astype(jnp.int32)
    padded = (cnt + te - 1) // te * te
    ends = jnp.cumsum(padded)
    offs = ends - padded
    slot1 = offs[i1] + r1
    slot2 = offs[i2] + r2
    s_max = (TOP_K * n + N_EXPERTS * (te - 1) + te - 1) // te * te
    tok = jnp.arange(n, dtype=jnp.int32)
    src = jnp.zeros((s_max,), jnp.int32).at[slot1].set(tok).at[slot2].set(tok)
    n_active = (ends[-1] // te).astype(jnp.int32).reshape(1)
    starts = jnp.arange(s_max // te, dtype=jnp.int32) * te
    tile_expert = jnp.minimum(jnp.searchsorted(ends, starts, side="right"), N_EXPERTS - 1).astype(jnp.int32)
    return src, jnp.concatenate([slot1, slot2]), tile_expert, n_active


def _prep_decay_params(a_log, dt_bias):
    rows = jnp.zeros((SUBLANES, LANES), F32)
    rows = rows.at[0, DN_HEADS:2 * DN_HEADS].set(a_log)
    rows = rows.at[1, DN_HEADS:2 * DN_HEADS].set(dt_bias)
    return rows


def _prep_gmlp_params(sp_w, sp_b, sample_len):
    reps = GMLP_CHUNK // sample_len
    w_s = jnp.tile(sp_w[:, :sample_len, :sample_len], (1, reps, reps))
    b_s = jnp.tile(sp_b[:, :sample_len], (1, reps))
    w2 = jnp.stack([sp_w, w_s])
    b2 = jnp.stack([sp_b.T, b_s.T])
    b2 = jnp.pad(b2, ((0, 0), (0, 0), (0, LANES - GMLP_GROUPS)))
    return w2, b2


def kernel(x_prompt, x_sample, state_conv, state_delta, norm_mix_g, w_in, conv_w, a_log, dt_bias, dn_norm_g,
           gm_ln_g, gm_ln_b, sp_w, sp_b, w_dn_out, w_gm_out, w_out, norm_ffn_g, ffn_wg, ffn_wu, ffn_wd,
           router_w, router_b, moe_wg, moe_wu, moe_wd, final_g):
    bp, tp, _ = x_prompt.shape
    bs, ts, _ = x_sample.shape
    n_p, n_s = bp * tp, bs * ts
    n = n_p + n_s
    assert tp % GMLP_CHUNK == 0 and GMLP_CHUNK % ts == 0 and n_s % GMLP_CHUNK == 0 and ts % CHUNK == 0

    x_parts = [x_prompt.reshape(n_p, D_MODEL), x_sample.reshape(n_s, D_MODEL)]
    w_in_t = jnp.swapaxes(w_in, 1, 2)
    conv_p, conv_s, delta_p, delta_s, v_s = [], [], [], [], []
    y_p = y_s = None
    for l in range(DEPTH):
        w_ba_t = jnp.pad(w_in_t[l, OFF_B:OFF_U, :], ((0, LANES - BA_W), (0, 0)))
        hn, p_ba = _mixnorm(x_parts, norm_mix_g[l].reshape(1, D_MODEL), w_ba_t)
        p_main = _inproj(hn, w_in_t, l)
        u, wq, lk, eg, tq, tk, tv = _delta_pre(p_main, p_ba, state_conv, l, conv_w[l],
                                               _prep_decay_params(a_log[l], dt_bias[l]), bp, tp, bs, ts)
        conv_new = jnp.concatenate([tq, tk, tv], axis=-1)[:, SUBLANES - (CONV_W - 1):, :]
        o, sp_out, ss_out = _delta_rec(u, wq, lk, eg, p_main, state_delta, l, dn_norm_g[l].reshape(1, DN_DV),
                                       bp, tp, bs, ts)
        w2, b2 = _prep_gmlp_params(sp_w[l], sp_b[l], ts)
        gm, vn_s = _gmlp(p_main, gm_ln_g[l].reshape(1, GMLP_W), gm_ln_b[l].reshape(1, GMLP_W), w2, b2, n_p, ts)
        y = _merge(o, gm, w_dn_out, w_gm_out, l, p_main)
        x = _outproj(y, w_out, l, x_parts)

        conv_p.append(conv_new[:bp])
        conv_s.append(conv_new[bp:])
        delta_p.append(sp_out)
        delta_s.append(ss_out)
        v_s.append(vn_s.reshape(bs, ts, GMLP_W))

        g_ffn = norm_ffn_g[l].reshape(1, D_MODEL)
        if l % 2 == 0:
            x = _ffn(x, g_ffn, ffn_wg[l // 2].astype(BF16), ffn_wu[l // 2].astype(BF16), ffn_wd[l // 2].astype(BF16))
            x_parts = [x]
        else:
            e = l // 2
            rw = jnp.pad(router_w[e], ((0, 0), (0, LANES - N_EXPERTS)))
            rb = jnp.pad(router_b[e], (0, LANES - N_EXPERTS), constant_values=-jnp.inf).reshape(1, LANES)
            meta, counts = _router(x, g_ffn, rw, rb)
            src, back, tile_expert, n_active = _routing_tables(meta, counts, n)
            ys3 = _experts(x.reshape(n, ROW_CHUNKS, LANES), src, g_ffn, moe_wg[e].astype(BF16),
                           moe_wu[e].astype(BF16), moe_wd[e].astype(BF16), tile_expert, n_active, EXPERT_TILE)
            if l == DEPTH - 1:
                fg = final_g.reshape(1, D_MODEL)
                y_p = _combine(x, ys3, back, meta, fg, 0, n_p)
                y_s = _combine(x, ys3, back, meta, fg, n_p, n_s)
    return (y_p.reshape(bp, tp, D_MODEL), y_s.reshape(bs, ts, D_MODEL), jnp.stack(conv_p), jnp.stack(delta_p),
            jnp.stack(conv_s), jnp.stack(delta_s), jnp.stack(v_s))
```

```python
import functools
import math

import jax
import jax.numpy as jnp
from jax import lax
from jax.experimental import pallas as pl
from jax.experimental.pallas import tpu as pltpu

F32 = jnp.float32
BF16 = jnp.bfloat16
HIGHEST = lax.Precision.HIGHEST

D_MODEL = 2048
DEPTH = 2
CHUNK = 64
DN_HEADS = 16
DN_DK = 128
DN_DV = 128
DN_QK_W = DN_HEADS * DN_DK
DN_V_W = DN_HEADS * DN_DV
DN_QKV_W = 2 * DN_QK_W + DN_V_W
CONV_W = 4
DN_SCALE = DN_DK ** -0.5
GMLP_CHUNK = 128
GMLP_GROUPS = 16
GMLP_GROUP_DIM = 128
GMLP_W = GMLP_GROUPS * GMLP_GROUP_DIM
OFF_Z = DN_QKV_W
OFF_B = OFF_Z + DN_V_W
OFF_A = OFF_B + DN_HEADS
OFF_U = OFF_A + DN_HEADS
OFF_V = OFF_U + GMLP_W
OFF_GA = OFF_V + GMLP_W
OFF_GB = OFF_GA + D_MODEL
IN_W = OFF_GB + D_MODEL
D_FF = 11 * D_MODEL // 4
N_EXPERTS = 8
TOP_K = 2
D_FF_EXPERT = D_FF // 2
RMS_EPS = 1e-6
LN_EPS = 1e-5
L2_EPS = 1e-6

LANES = 128
SUBLANES = 8
V7X_VMEM_LIMIT = 56 * 1024 * 1024
SMALL_VMEM_LIMIT = 32 * 1024 * 1024

MAIN_W = IN_W - 2 * DN_HEADS
COL_Z = OFF_Z
COL_U = COL_Z + DN_V_W
COL_V = COL_U + GMLP_W
COL_GA = COL_V + GMLP_W
COL_GB = COL_GA + D_MODEL

ROW_CHUNKS = D_MODEL // LANES
N_PAIRS = DN_HEADS // 2
PAIR_W = 2 * DN_DK
PREV_ROWS = 16
PAIR_GROUP = 8


def _tile(n, pref):
    t = pref
    while n % t:
        t //= 2
    return t


def _cparams(sem, vmem_bytes=V7X_VMEM_LIMIT):
    return pltpu.CompilerParams(dimension_semantics=sem, vmem_limit_bytes=vmem_bytes)


def _rms(x, g):
    ms = jnp.mean(x * x, axis=-1, keepdims=True)
    return x * lax.rsqrt(ms + RMS_EPS) * g


def _dot(a, b):
    return jnp.dot(a, b, preferred_element_type=F32)


def _dot_nt(a, b):
    return lax.dot_general(a, b, (((1,), (1,)), ((), ())), preferred_element_type=F32)


def _silu(x):
    return x * jax.nn.sigmoid(x)


def _gelu(x):
    return 0.5 * x * (1.0 + jnp.tanh(math.sqrt(2.0 / math.pi) * (x + 0.044715 * (x * x * x))))


def _softplus(x):
    return jnp.maximum(x, 0.0) + jnp.log1p(jnp.exp(-jnp.abs(x)))


BA_W = 2 * DN_HEADS


def _row_sources(parts, block_shape, index_fn):
    if len(parts) == 1:
        return [pl.BlockSpec(block_shape, index_fn)], 0
    tm = block_shape[0]
    assert len(parts) == 2 and all(p.shape[0] % tm == 0 for p in parts)
    t0 = parts[0].shape[0] // tm

    def first(*ids):
        r, c = index_fn(*ids)
        return jnp.minimum(r, t0 - 1), c

    def second(*ids):
        r, c = index_fn(*ids)
        return jnp.maximum(r - t0, 0), c

    return [pl.BlockSpec(block_shape, first), pl.BlockSpec(block_shape, second)], t0


def _pick_rows(refs, first_tiles, row_axis):
    if len(refs) == 1:
        return refs[0][...]
    return jnp.where(pl.program_id(row_axis) < first_tiles, refs[0][...], refs[1][...])


def _mixnorm_kernel(*refs, first_tiles):
    *x_refs, g_ref, wbat_ref, hn_ref, pba_ref = refs
    hn = _rms(_pick_rows(x_refs, first_tiles, 0), g_ref[...]).astype(BF16)
    hn_ref[...] = hn
    pba_ref[...] = _dot_nt(hn, wbat_ref[...].astype(BF16))


def _mixnorm(x_parts, g, w_ba_t):
    n = sum(p.shape[0] for p in x_parts)
    tm = _tile(math.gcd(*[p.shape[0] for p in x_parts]), 512)
    x_specs, first_tiles = _row_sources(x_parts, (tm, D_MODEL), lambda i: (i, 0))
    return pl.pallas_call(
        functools.partial(_mixnorm_kernel, first_tiles=first_tiles),
        grid=(n // tm,),
        in_specs=x_specs + [
            pl.BlockSpec((1, D_MODEL), lambda i: (0, 0)),
            pl.BlockSpec((LANES, D_MODEL), lambda i: (0, 0)),
        ],
        out_specs=[
            pl.BlockSpec((tm, D_MODEL), lambda i: (i, 0)),
            pl.BlockSpec((tm, LANES), lambda i: (i, 0)),
        ],
        out_shape=[jax.ShapeDtypeStruct((n, D_MODEL), BF16), jax.ShapeDtypeStruct((n, LANES), F32)],
        compiler_params=_cparams(("parallel",), SMALL_VMEM_LIMIT),
        name="mix_norm",
    )(*x_parts, g, w_ba_t)


def _inproj_kernel(hn_ref, wa_ref, wb_ref, p_ref, w16_ref, *, first_shifted):
    j = pl.program_id(0)

    @pl.when(pl.program_id(1) == 0)
    def _():
        @pl.when(j < first_shifted)
        def _():
            w16_ref[...] = wa_ref[...].T.astype(BF16)

        @pl.when(j >= first_shifted)
        def _():
            wt = jnp.concatenate([wa_ref[BA_W:, :], wb_ref[:BA_W, :]], axis=0)
            w16_ref[...] = wt.T.astype(BF16)

    p_ref[...] = _dot(hn_ref[...], w16_ref[...]).astype(p_ref.dtype)


def _inproj(hn, w_in_t, layer):
    n = hn.shape[0]
    tm = _tile(n, 1024)
    tn = 1024
    assert OFF_B % tn == 0 and OFF_U - OFF_B == BA_W and BA_W % SUBLANES == 0
    first_shifted = OFF_B // tn
    tail_blocks = tn // LANES
    kern = functools.partial(_inproj_kernel, first_shifted=first_shifted)
    return pl.pallas_call(
        kern,
        grid=(MAIN_W // tn, n // tm),
        in_specs=[
            pl.BlockSpec((tm, D_MODEL), lambda j, i: (i, 0)),
            pl.BlockSpec((None, tn, D_MODEL), lambda j, i: (layer, j, 0)),
            pl.BlockSpec((None, LANES, D_MODEL), lambda j, i: (layer, (j + 1) * tail_blocks, 0)),
        ],
        out_specs=pl.BlockSpec((tm, tn), lambda j, i: (i, j)),
        out_shape=jax.ShapeDtypeStruct((n, MAIN_W), BF16),
        scratch_shapes=[pltpu.VMEM((D_MODEL, tn), BF16)],
        compiler_params=_cparams(("parallel", "arbitrary")),
        name="in_proj",
    )(hn, w_in_t, w_in_t)


def _dpre_kernel(q_ref, k_ref, v_ref, pq_ref, pk_ref, pv_ref, cq_ref, ck_ref, cv_ref, cwq_ref, cwk_ref, cwv_ref,
                 ba_ref, ab_ref, u_ref, wqo_ref, lk_ref, eg_ref, tq_ref, tk_ref, tv_ref, xbuf_ref, cbuf_ref, cnext_ref,
                 *, nb, npb, ncp, ncs):
    t = pl.program_id(0)
    s = jnp.minimum(t, nb - 1)
    is_p = s < npb
    first = jnp.where(is_p, s % ncp, (s - npb) % ncs) == 0
    hist = SUBLANES - (CONV_W - 1)

    @pl.when(t == 0)
    def _():
        cbuf_ref[...] = jnp.zeros_like(cbuf_ref)

    def conv_silu(j, raw_ref, prev_ref, cp_ref, cw_ref, tail_ref):
        raw = raw_ref[...].astype(F32)
        tail_ref[0] = raw[CHUNK - SUBLANES:, :]
        tail = prev_ref[prev_ref.shape[0] - SUBLANES:, :].astype(F32)
        xbuf_ref[j] = tail
        xbuf_ref[j, hist:SUBLANES, :] = jnp.where(first, jnp.where(is_p, 0.0, cp_ref[0]), tail[hist:, :])
        rows = jnp.concatenate([xbuf_ref[j], raw], axis=0)
        acc = raw * cw_ref[CONV_W - 1:CONV_W, :]
        for sft in range(1, CONV_W):
            acc = acc + pltpu.roll(rows, sft, axis=0)[SUBLANES:] * cw_ref[CONV_W - 1 - sft:CONV_W - sft, :]
        return _silu(acc)

    ba = ba_ref[...]
    lane = lax.broadcasted_iota(jnp.int32, (CHUNK, LANES), 1)
    gval = -jnp.exp(ab_ref[0:1, :]) * _softplus(ba + ab_ref[1:2, :])
    bg = jnp.where(lane < DN_HEADS, jax.nn.sigmoid(ba), gval)
    r64 = lax.broadcasted_iota(jnp.int32, (CHUNK, CHUNK), 0)
    c64 = lax.broadcasted_iota(jnp.int32, (CHUNK, CHUNK), 1)
    gc_cols = jnp.dot((r64 >= c64).astype(F32), bg, precision=HIGHEST, preferred_element_type=F32)
    gc_rows = gc_cols.T
    gc_rows2 = jnp.concatenate([gc_rows, gc_rows], axis=1)

    ri = lax.broadcasted_iota(jnp.int32, (2 * CHUNK, 2 * CHUNK), 0)
    ci = lax.broadcasted_iota(jnp.int32, (2 * CHUNK, 2 * CHUNK), 1)
    same = (ri // CHUNK) == (ci // CHUNK)
    causal = same & (ri >= ci)
    strict = same & (ri > ci)
    left = lax.broadcasted_iota(jnp.int32, (1, 2 * CHUNK), 1) < CHUNK

    def stack(j, p):
        return jnp.concatenate([cbuf_ref[j, :, (2 * p) * DN_DK:(2 * p + 1) * DN_DK],
                                cbuf_ref[j, :, (2 * p + 1) * DN_DK:(2 * p + 2) * DN_DK]], axis=0)

    def colpair(a, off, p):
        return jnp.concatenate([a[:, off + 2 * p:off + 2 * p + 1], a[:, off + 2 * p + 1:off + 2 * p + 2]], axis=0)

    def pair_group(pairs):
        beta = {p: colpair(bg, 0, p) for p in pairs}
        gcc = {p: colpair(gc_cols, DN_HEADS, p) for p in pairs}
        gcr = {p: jnp.where(left, gc_rows2[DN_HEADS + 2 * p:DN_HEADS + 2 * p + 1, :],
                            gc_rows2[DN_HEADS + 2 * p + 1:DN_HEADS + 2 * p + 2, :]) for p in pairs}
        gl = {p: [gc_cols[CHUNK - 1:CHUNK, DN_HEADS + 2 * p + t:DN_HEADS + 2 * p + t + 1] for t in (0, 1)]
              for p in pairs}
        glr = {p: jnp.where(left, gl[p][0], gl[p][1]) for p in pairs}
        decay = {p: jnp.exp(jnp.where(causal, gcc[p] - gcr[p], -jnp.inf)) for p in pairs}
        egc = {p: jnp.exp(gcc[p]) for p in pairs}

        qh = {p: stack(0, p) for p in pairs}
        kh = {p: stack(1, p) for p in pairs}
        q = {p: qh[p] * (lax.rsqrt(jnp.sum(qh[p] * qh[p], axis=-1, keepdims=True) + L2_EPS) * DN_SCALE) for p in pairs}
        k = {p: kh[p] * lax.rsqrt(jnp.sum(kh[p] * kh[p], axis=-1, keepdims=True) + L2_EPS) for p in pairs}
        kb = {p: k[p] * beta[p] for p in pairs}
        k16 = {p: k[p].astype(BF16) for p in pairs}

        m = {p: jnp.where(strict, -(_dot_nt(kb[p].astype(BF16), k16[p]) * decay[p]), 0.0) for p in pairs}
        r = dict(m)
        for _ in range(5):
            m16 = {p: m[p].astype(BF16) for p in pairs}
            m = {p: _dot(m16[p], m16[p]) for p in pairs}
            r = {p: r[p] + m[p] + _dot(r[p].astype(BF16), m[p].astype(BF16)) for p in pairs}
        rhs = {p: jnp.concatenate([stack(2, p) * beta[p], kb[p] * egc[p]], axis=1) for p in pairs}
        uw = {p: rhs[p] + _dot(r[p].astype(BF16), rhs[p].astype(BF16)) for p in pairs}
        qk = {p: _dot_nt(q[p].astype(BF16), k16[p]) * decay[p] for p in pairs}
        kdt = {p: k[p].T * jnp.exp(glr[p] - gcr[p]) for p in pairs}

        for p in pairs:
            for t in (0, 1):
                sl = slice((2 * p + t) * DN_DK, (2 * p + t + 1) * DN_DK)
                rows = slice(t * CHUNK, (t + 1) * CHUNK)
                u_ref[:, sl] = uw[p][rows, :DN_DV]
                wqo_ref[0:CHUNK, sl] = uw[p][rows, DN_DV:].astype(BF16)
                wqo_ref[CHUNK:2 * CHUNK, sl] = (q[p][rows] * egc[p][rows]).astype(BF16)
                eg_ref[:, sl] = jnp.broadcast_to(jnp.exp(gl[p][t]), (SUBLANES, DN_DV))
            psl = slice(p * 2 * CHUNK, (p + 1) * 2 * CHUNK)
            lk_ref[0:CHUNK, psl] = jnp.where(left, qk[p][:CHUNK], qk[p][CHUNK:]).astype(BF16)
            lk_ref[CHUNK:3 * CHUNK, psl] = kdt[p].astype(BF16)

    for first_pair in range(0, N_PAIRS, PAIR_GROUP):
        pair_group(range(first_pair, first_pair + PAIR_GROUP))

    cnext_ref[0] = conv_silu(0, q_ref, pq_ref, cq_ref, cwq_ref, tq_ref)
    cnext_ref[1] = conv_silu(1, k_ref, pk_ref, ck_ref, cwk_ref, tk_ref)
    cnext_ref[2] = conv_silu(2, v_ref, pv_ref, cv_ref, cwv_ref, tv_ref)
    cbuf_ref[...] = cnext_ref[...]


def _delta_pre(p_main, p_ba, state_conv, layer, conv_w, ab, bp, tp, bs, ts):
    n = p_main.shape[0]
    ncp, ncs = tp // CHUNK, ts // CHUNK
    npb = bp * ncp
    nb = n // CHUNK
    rows_per_blk = CHUNK // PREV_ROWS

    def conv_blk(t):
        return jnp.minimum(t, nb - 1)

    def mat_blk(t):
        return jnp.maximum(t - 1, 0)

    def seq_s(t):
        return jnp.maximum(conv_blk(t) - npb, 0) // ncs

    def seq_all(t):
        blk = conv_blk(t)
        return jnp.where(blk < npb, blk // ncp, bp + (blk - npb) // ncs)

    def col(c):
        return pl.BlockSpec((CHUNK, DN_QK_W), lambda t, c=c: (conv_blk(t), c))

    def prev(c):
        return pl.BlockSpec((PREV_ROWS, DN_QK_W),
                            lambda t, c=c: (jnp.maximum(conv_blk(t) * rows_per_blk - 1, 0), c))

    def cprev(c):
        return pl.BlockSpec((None, 1, CONV_W - 1, DN_QK_W), lambda t, c=c: (layer, seq_s(t), 0, c))

    def cw(c):
        return pl.BlockSpec((CONV_W, DN_QK_W), lambda t, c=c: (0, c))

    kern = functools.partial(_dpre_kernel, nb=nb, npb=npb, ncp=ncp, ncs=ncs)
    return pl.pallas_call(
        kern,
        grid=(nb + 1,),
        in_specs=[
            col(0), col(1), col(2), prev(0), prev(1), prev(2), cprev(0), cprev(1), cprev(2), cw(0), cw(1), cw(2),
            pl.BlockSpec((CHUNK, LANES), lambda t: (mat_blk(t), 0)),
            pl.BlockSpec((SUBLANES, LANES), lambda t: (0, 0)),
        ],
        out_specs=[
            pl.BlockSpec((CHUNK, DN_V_W), lambda t: (mat_blk(t), 0)),
            pl.BlockSpec((2 * CHUNK, DN_QK_W), lambda t: (mat_blk(t), 0)),
            pl.BlockSpec((3 * CHUNK, DN_HEADS * CHUNK), lambda t: (mat_blk(t), 0)),
            pl.BlockSpec((SUBLANES, DN_V_W), lambda t: (mat_blk(t), 0)),
        ] + [pl.BlockSpec((1, SUBLANES, DN_QK_W), lambda t: (seq_all(t), 0, 0))] * 3,
        out_shape=[
            jax.ShapeDtypeStruct((n, DN_V_W), F32),
            jax.ShapeDtypeStruct((2 * n, DN_QK_W), BF16),
            jax.ShapeDtypeStruct((3 * n, DN_HEADS * CHUNK), BF16),
            jax.ShapeDtypeStruct((nb * SUBLANES, DN_V_W), F32),
        ] + [jax.ShapeDtypeStruct((bp + bs, SUBLANES, DN_QK_W), F32)] * 3,
        scratch_shapes=[pltpu.VMEM((3, SUBLANES, DN_QK_W), F32), pltpu.VMEM((3, CHUNK, DN_QK_W), F32),
                        pltpu.VMEM((3, CHUNK, DN_QK_W), F32)],
        compiler_params=_cparams(("arbitrary",), SMALL_VMEM_LIMIT),
        name="delta_pre",
    )(p_main, p_main, p_main, p_main, p_main, p_main, state_conv, state_conv, state_conv, conv_w, conv_w, conv_w,
      p_ba, ab)


def _drec_kernel(u_ref, wq_ref, lk_ref, eg_ref, z_ref, sp_ref, dng_ref, o_ref, sop_ref, sos_ref, st_ref,
                 *, npb, ncp, ncs):
    s = pl.program_id(0)
    is_p = s < npb
    cidx = jnp.where(is_p, s % ncp, (s - npb) % ncs)
    first = cidx == 0
    last = cidx == jnp.where(is_p, ncp, ncs) - 1

    @pl.when(first)
    def _():
        for h in range(DN_HEADS):
            st_ref[:, h * DN_DV:(h + 1) * DN_DV] = jnp.where(is_p, 0.0, sp_ref[0, h])

    lane = lax.broadcasted_iota(jnp.int32, (1, PAIR_W), 1)
    left = lane < DN_DV
    dng = dng_ref[...]
    for p in range(N_PAIRS):
        psl = slice(p * PAIR_W, (p + 1) * PAIR_W)
        st = st_ref[:, psl]
        st16 = st.astype(BF16)
        zero = jnp.zeros_like(st16)
        sbd = jnp.concatenate([jnp.where(left, st16, zero), jnp.where(left, zero, st16)], axis=0)
        ws = _dot(wq_ref[:, psl], sbd)
        v_new = (u_ref[:, psl] - ws[:CHUNK]).astype(BF16)
        vzero = jnp.zeros_like(v_new)
        vbd = jnp.concatenate([jnp.where(left, v_new, vzero), jnp.where(left, vzero, v_new)], axis=0)
        t = _dot(lk_ref[:, p * 2 * CHUNK:(p + 1) * 2 * CHUNK], vbd)
        o = ws[CHUNK:] + t[:CHUNK]
        st_ref[:, psl] = st * eg_ref[0:1, psl] + t[CHUNK:]
        for hh in (0, 1):
            sl = slice(p * PAIR_W + hh * DN_DV, p * PAIR_W + (hh + 1) * DN_DV)
            oh = o[:, hh * DN_DV:(hh + 1) * DN_DV]
            o_ref[:, sl] = (_rms(oh, dng) * _silu(z_ref[:, sl].astype(F32))).astype(o_ref.dtype)

    @pl.when(last & is_p)
    def _():
        for h in range(DN_HEADS):
            sop_ref[0, h] = st_ref[:, h * DN_DV:(h + 1) * DN_DV]

    @pl.when(last & jnp.logical_not(is_p))
    def _():
        for h in range(DN_HEADS):
            sos_ref[0, h] = st_ref[:, h * DN_DV:(h + 1) * DN_DV]


def _delta_rec(u, wq, lk, eg, p_main, state_delta, layer, dn_g, bp, tp, bs, ts):
    n = u.shape[0]
    ncp, ncs = tp // CHUNK, ts // CHUNK
    npb = bp * ncp
    nb = n // CHUNK
    zblk = COL_Z // DN_V_W

    def seq_s(s):
        return jnp.maximum(s - npb, 0) // ncs

    def seq_p(s):
        return jnp.minimum(s // ncp, bp - 1)

    kern = functools.partial(_drec_kernel, npb=npb, ncp=ncp, ncs=ncs)
    return pl.pallas_call(
        kern,
        grid=(nb,),
        in_specs=[
            pl.BlockSpec((CHUNK, DN_V_W), lambda s: (s, 0)),
            pl.BlockSpec((2 * CHUNK, DN_QK_W), lambda s: (s, 0)),
            pl.BlockSpec((3 * CHUNK, DN_HEADS * CHUNK), lambda s: (s, 0)),
            pl.BlockSpec((SUBLANES, DN_V_W), lambda s: (s, 0)),
            pl.BlockSpec((CHUNK, DN_V_W), lambda s: (s, zblk)),
            pl.BlockSpec((None, 1, DN_HEADS, DN_DK, DN_DV), lambda s: (layer, seq_s(s), 0, 0, 0)),
            pl.BlockSpec((1, DN_DV), lambda s: (0, 0)),
        ],
        out_specs=[
            pl.BlockSpec((CHUNK, DN_V_W), lambda s: (s, 0)),
            pl.BlockSpec((1, DN_HEADS, DN_DK, DN_DV), lambda s: (seq_p(s), 0, 0, 0)),
            pl.BlockSpec((1, DN_HEADS, DN_DK, DN_DV), lambda s: (seq_s(s), 0, 0, 0)),
        ],
        out_shape=[
            jax.ShapeDtypeStruct((n, DN_V_W), BF16),
            jax.ShapeDtypeStruct((bp, DN_HEADS, DN_DK, DN_DV), F32),
            jax.ShapeDtypeStruct((bs, DN_HEADS, DN_DK, DN_DV), F32),
        ],
        scratch_shapes=[pltpu.VMEM((DN_DK, DN_V_W), F32)],
        compiler_params=_cparams(("arbitrary",), SMALL_VMEM_LIMIT),
        name="delta_rec",
    )(u, wq, lk, eg, p_main, state_delta, dn_g)


def _gmlp_kernel(u_ref, v_ref, lg_ref, lb_ref, w_ref, b_ref, gm_ref, vn_ref, *, npb, sample_len):
    s = pl.program_id(0)
    is_s = s >= npb
    gu = _gelu(u_ref[...].astype(F32))
    gv = _gelu(v_ref[...].astype(F32))
    xc = gv - jnp.mean(gv, axis=-1, keepdims=True)
    var = jnp.mean(xc * xc, axis=-1, keepdims=True)
    vn = xc * lax.rsqrt(var + LN_EPS) * lg_ref[...] + lb_ref[...]

    @pl.when(is_s)
    def _():
        vn_ref[...] = vn

    ri = lax.broadcasted_iota(jnp.int32, (GMLP_CHUNK, GMLP_CHUNK), 0)
    ci = lax.broadcasted_iota(jnp.int32, (GMLP_CHUNK, GMLP_CHUNK), 1)
    same_seq = (ri // sample_len) == (ci // sample_len)
    mask = (ri >= ci) & (same_seq | jnp.logical_not(is_s))
    bias = b_ref[0]
    for g in range(GMLP_GROUPS):
        sl = slice(g * GMLP_GROUP_DIM, (g + 1) * GMLP_GROUP_DIM)
        wl = jnp.where(mask, w_ref[0, g], 0.0).astype(BF16)
        mixed = _dot(wl, vn[:, sl].astype(BF16)) + bias[:, g:g + 1]
        gm_ref[:, sl] = (gu[:, sl] * mixed).astype(gm_ref.dtype)


def _gmlp(p_main, ln_g, ln_b, w2, b2, n_prompt, sample_len):
    n = p_main.shape[0]
    nb = n // GMLP_CHUNK
    npb = n_prompt // GMLP_CHUNK
    ublk = COL_U // GMLP_W
    vblk = COL_V // GMLP_W
    kern = functools.partial(_gmlp_kernel, npb=npb, sample_len=sample_len)
    return pl.pallas_call(
        kern,
        grid=(nb,),
        in_specs=[
            pl.BlockSpec((GMLP_CHUNK, GMLP_W), lambda s: (s, ublk)),
            pl.BlockSpec((GMLP_CHUNK, GMLP_W), lambda s: (s, vblk)),
            pl.BlockSpec((1, GMLP_W), lambda s: (0, 0)),
            pl.BlockSpec((1, GMLP_W), lambda s: (0, 0)),
            pl.BlockSpec((1, GMLP_GROUPS, GMLP_CHUNK, GMLP_CHUNK), lambda s: (jnp.where(s >= npb, 1, 0), 0, 0, 0)),
            pl.BlockSpec((1, GMLP_CHUNK, LANES), lambda s: (jnp.where(s >= npb, 1, 0), 0, 0)),
        ],
        out_specs=[
            pl.BlockSpec((GMLP_CHUNK, GMLP_W), lambda s: (s, 0)),
            pl.BlockSpec((GMLP_CHUNK, GMLP_W), lambda s: (jnp.maximum(s - npb, 0), 0)),
        ],
        out_shape=[
            jax.ShapeDtypeStruct((n, GMLP_W), BF16),
            jax.ShapeDtypeStruct((n - n_prompt, GMLP_W), F32),
        ],
        compiler_params=_cparams(("arbitrary",), SMALL_VMEM_LIMIT),
        name="gmlp",
    )(p_main, p_main, ln_g, ln_b, w2, b2)


def _merge_kernel(o_ref, gm_ref, wd_ref, wg_ref, ga_ref, gb_ref, y_ref, wd16_ref, wg16_ref):
    @pl.when(pl.program_id(1) == 0)
    def _():
        wd16_ref[...] = wd_ref[...].astype(BF16)
        wg16_ref[...] = wg_ref[...].astype(BF16)

    a = _dot(o_ref[...], wd16_ref[...])
    b = _dot(gm_ref[...], wg16_ref[...])
    ga = jax.nn.sigmoid(ga_ref[...].astype(F32))
    gb = jax.nn.sigmoid(gb_ref[...].astype(F32))
    y_ref[...] = (ga * a + gb * b).astype(y_ref.dtype)


def _merge(o, gm, w_dn, w_gm, layer, p_main):
    n = o.shape[0]
    tm = _tile(n, 256)
    tn = 1024
    ga0, gb0 = COL_GA // tn, COL_GB // tn
    return pl.pallas_call(
        _merge_kernel,
        grid=(D_MODEL // tn, n // tm),
        in_specs=[
            pl.BlockSpec((tm, DN_V_W), lambda j, i: (i, 0)),
            pl.BlockSpec((tm, GMLP_W), lambda j, i: (i, 0)),
            pl.BlockSpec((None, DN_V_W, tn), lambda j, i: (layer, 0, j)),
            pl.BlockSpec((None, GMLP_W, tn), lambda j, i: (layer, 0, j)),
            pl.BlockSpec((tm, tn), lambda j, i: (i, ga0 + j)),
            pl.BlockSpec((tm, tn), lambda j, i: (i, gb0 + j)),
        ],
        out_specs=pl.BlockSpec((tm, tn), lambda j, i: (i, j)),
        out_shape=jax.ShapeDtypeStruct((n, D_MODEL), BF16),
        scratch_shapes=[pltpu.VMEM((DN_V_W, tn), BF16), pltpu.VMEM((GMLP_W, tn), BF16)],
        compiler_params=_cparams(("parallel", "arbitrary")),
        name="merge",
    )(o, gm, w_dn, w_gm, p_main, p_main)


def _outproj_kernel(y_ref, w_ref, *refs, first_tiles):
    *x_refs, o_ref, w16_ref = refs

    @pl.when(pl.program_id(1) == 0)
    def _():
        w16_ref[...] = w_ref[...].astype(BF16)

    o_ref[...] = _pick_rows(x_refs, first_tiles, 1) + _dot(y_ref[...], w16_ref[...])


def _outproj(y, w, layer, x_parts):
    n = y.shape[0]
    tm = _tile(math.gcd(*[p.shape[0] for p in x_parts]), 512)
    tn = 1024
    x_specs, first_tiles = _row_sources(x_parts, (tm, tn), lambda j, i: (i, j))
    return pl.pallas_call(
        functools.partial(_outproj_kernel, first_tiles=first_tiles),
        grid=(D_MODEL // tn, n // tm),
        in_specs=[
            pl.BlockSpec((tm, D_MODEL), lambda j, i: (i, 0)),
            pl.BlockSpec((None, D_MODEL, tn), lambda j, i: (layer, 0, j)),
        ] + x_specs,
        out_specs=pl.BlockSpec((tm, tn), lambda j, i: (i, j)),
        out_shape=jax.ShapeDtypeStruct((n, D_MODEL), F32),
        scratch_shapes=[pltpu.VMEM((D_MODEL, tn), BF16)],
        compiler_params=_cparams(("parallel", "arbitrary")),
        name="out_proj",
    )(y, w, *x_parts)


def _ffn_kernel(x_ref, g_ref, wg_ref, wu_ref, wd_ref, o_ref, hn_ref):
    @pl.when(pl.program_id(1) == 0)
    def _():
        x = x_ref[...]
        hn_ref[...] = _rms(x, g_ref[...]).astype(BF16)
        o_ref[...] = x

    hn = hn_ref[...]
    hid = (_silu(_dot(hn, wg_ref[...])) * _dot(hn, wu_ref[...])).astype(BF16)
    o_ref[...] += _dot(hid, wd_ref[...])


def _ffn(x, g, wg, wu, wd):
    n = x.shape[0]
    tm = _tile(n, 1024)
    tf = 512
    return pl.pallas_call(
        _ffn_kernel,
        grid=(n // tm, D_FF // tf),
        in_specs=[
            pl.BlockSpec((tm, D_MODEL), lambda i, j: (i, 0)),
            pl.BlockSpec((1, D_MODEL), lambda i, j: (0, 0)),
            pl.BlockSpec((D_MODEL, tf), lambda i, j: (0, j)),
            pl.BlockSpec((D_MODEL, tf), lambda i, j: (0, j)),
            pl.BlockSpec((tf, D_MODEL), lambda i, j: (j, 0)),
        ],
        out_specs=pl.BlockSpec((tm, D_MODEL), lambda i, j: (i, 0)),
        out_shape=jax.ShapeDtypeStruct((n, D_MODEL), F32),
        scratch_shapes=[pltpu.VMEM((tm, D_MODEL), BF16)],
        compiler_params=_cparams(("parallel", "arbitrary")),
        name="ffn_dense",
    )(x, g, wg, wu, wd)


def _router_kernel(x_ref, g_ref, rw_ref, rb_ref, meta_ref, cnt_ref, carry_ref):
    @pl.when(pl.program_id(0) == 0)
    def _():
        carry_ref[...] = jnp.zeros_like(carry_ref)

    tm = x_ref.shape[0]
    hn = _rms(x_ref[...], g_ref[...])
    logits = jnp.dot(hn, rw_ref[...], precision=HIGHEST, preferred_element_type=F32) + rb_ref[...]
    lane = lax.broadcasted_iota(jnp.int32, (tm, LANES), 1).astype(F32)
    m1 = jnp.max(logits, axis=-1, keepdims=True)
    i1 = jnp.min(jnp.where(logits == m1, lane, float(LANES)), axis=-1, keepdims=True)
    oh1 = lane == i1
    rest = jnp.where(oh1, -jnp.inf, logits)
    m2 = jnp.max(rest, axis=-1, keepdims=True)
    i2 = jnp.min(jnp.where(rest == m2, lane, float(LANES)), axis=-1, keepdims=True)
    oh2 = lane == i2
    e = jnp.exp(m2 - m1)
    p1 = 1.0 / (1.0 + e)
    p2 = e / (1.0 + e)

    onehot = jnp.where(oh1 | oh2, 1.0, 0.0)
    ri = lax.broadcasted_iota(jnp.int32, (tm, tm), 0)
    ci = lax.broadcasted_iota(jnp.int32, (tm, tm), 1)
    before = jnp.where(ri > ci, 1.0, 0.0).astype(BF16)
    rank = _dot(before, onehot.astype(BF16)) + carry_ref[...]
    r1 = jnp.sum(jnp.where(oh1, rank, 0.0), axis=-1, keepdims=True)
    r2 = jnp.sum(jnp.where(oh2, rank, 0.0), axis=-1, keepdims=True)
    carry_ref[...] += jnp.sum(onehot, axis=0, keepdims=True)

    meta = jnp.zeros((tm, LANES), F32)
    for idx, val in enumerate((i1, i2, p1, p2, r1, r2)):
        meta = jnp.where(lane == float(idx), val, meta)
    meta_ref[...] = meta
    cnt_ref[...] = carry_ref[...]


def _router(x, g, rw, rb):
    n = x.shape[0]
    tm = _tile(n, 256)
    return pl.pallas_call(
        _router_kernel,
        grid=(n // tm,),
        in_specs=[
            pl.BlockSpec((tm, D_MODEL), lambda i: (i, 0)),
            pl.BlockSpec((1, D_MODEL), lambda i: (0, 0)),
            pl.BlockSpec((D_MODEL, LANES), lambda i: (0, 0)),
            pl.BlockSpec((1, LANES), lambda i: (0, 0)),
        ],
        out_specs=[
            pl.BlockSpec((tm, LANES), lambda i: (i, 0)),
            pl.BlockSpec((1, LANES), lambda i: (0, 0)),
        ],
        out_shape=[jax.ShapeDtypeStruct((n, LANES), F32), jax.ShapeDtypeStruct((1, LANES), F32)],
        scratch_shapes=[pltpu.VMEM((1, LANES), F32)],
        compiler_params=_cparams(("arbitrary",), SMALL_VMEM_LIMIT),
        name="moe_router",
    )(x, g, rw, rb)


DMA_UNROLL = 32


def _gather_tile(idx_ref, idx_base, rows, src_ref, dst_ref, dst_base, sem, *, wait):
    def body(r, carry):
        cp = pltpu.make_async_copy(src_ref.at[idx_ref[idx_base + r]], dst_ref.at[dst_base + r], sem)
        if wait:
            cp.wait()
        else:
            cp.start()
        return carry
    lax.fori_loop(0, rows, body, 0, unroll=DMA_UNROLL)


def _cm_stride(rows):
    return rows + SUBLANES


def _rows_to_chunk_major(rows_ref, cm_ref, rows, stride):
    def body(r, carry):
        cm_ref[pl.ds(r, ROW_CHUNKS, stride=stride), :] = rows_ref[r]
        return carry
    lax.fori_loop(0, rows, body, 0, unroll=DMA_UNROLL)


def _chunk_major_to_rows(cm_ref, rows_ref, rows, stride):
    def body(r, carry):
        rows_ref[r] = cm_ref[pl.ds(r, ROW_CHUNKS, stride=stride), :]
        return carry
    lax.fori_loop(0, rows, body, 0, unroll=DMA_UNROLL)


def _expert_kernel(src_ref, te_ref, na_ref, x3_ref, g_ref, wg_ref, wu_ref, wd_ref, o_ref, buf_ref, hn_ref, cm_ref,
                   sem):
    i = pl.program_id(0)
    j = pl.program_id(1)
    nj = pl.num_programs(1)
    rows = hn_ref.shape[0]
    stride = cm_ref.shape[0] // ROW_CHUNKS
    n_active = na_ref[0]
    active = i < n_active
    slot = i % 2

    def chunk(c):
        return pl.ds(c * stride, rows)

    @pl.when((j == 0) & active)
    def _():
        @pl.when(i == 0)
        def _():
            _gather_tile(src_ref, 0, rows, x3_ref, buf_ref.at[0], 0, sem.at[0], wait=False)

        @pl.when(i + 1 < n_active)
        def _():
            _gather_tile(src_ref, (i + 1) * rows, rows, x3_ref, buf_ref.at[1 - slot], 0, sem.at[1 - slot], wait=False)

        _gather_tile(src_ref, i * rows, rows, x3_ref, buf_ref.at[slot], 0, sem.at[slot], wait=True)
        _rows_to_chunk_major(buf_ref.at[slot], cm_ref, rows, stride)

        ss = jnp.zeros((rows, 1), F32)
        for c in range(ROW_CHUNKS):
            xc = cm_ref[chunk(c), :]
            ss = ss + jnp.sum(xc * xc, axis=-1, keepdims=True)
        scale = lax.rsqrt(ss / D_MODEL + RMS_EPS)
        for c in range(ROW_CHUNKS):
            csl = slice(c * LANES, (c + 1) * LANES)
            hn_ref[:, csl] = (cm_ref[chunk(c), :] * scale * g_ref[:, csl]).astype(BF16)

    @pl.when(j == 0)
    def _():
        cm_ref[...] = jnp.zeros_like(cm_ref)

    @pl.when(active)
    def _():
        hn = hn_ref[...]
        hid = (_silu(_dot(hn, wg_ref[0].astype(BF16))) * _dot(hn, wu_ref[0].astype(BF16))).astype(BF16)
        out = _dot(hid, wd_ref[0].astype(BF16))
        for c in range(ROW_CHUNKS):
            cm_ref[chunk(c), :] += out[:, c * LANES:(c + 1) * LANES]

    @pl.when(j == nj - 1)
    def _():
        _chunk_major_to_rows(cm_ref, o_ref, rows, stride)


def _experts(x3, src, g, wg, wu, wd, tile_expert, n_active, te_rows):
    s_max = src.shape[0]
    n_tiles = s_max // te_rows
    tf = 256
    nj = D_FF_EXPERT // tf

    def row(i, na):
        return jnp.minimum(i, na[0] - 1)

    def jj(i, j, na):
        return jnp.where(i < na[0], j, nj - 1)

    return pl.pallas_call(
        _expert_kernel,
        grid_spec=pltpu.PrefetchScalarGridSpec(
            num_scalar_prefetch=3,
            grid=(n_tiles, nj),
            in_specs=[
                pl.BlockSpec(memory_space=pl.ANY),
                pl.BlockSpec((1, D_MODEL), lambda i, j, sr, te, na: (0, 0)),
                pl.BlockSpec((1, D_MODEL, tf), lambda i, j, sr, te, na: (te[row(i, na)], 0, jj(i, j, na))),
                pl.BlockSpec((1, D_MODEL, tf), lambda i, j, sr, te, na: (te[row(i, na)], 0, jj(i, j, na))),
                pl.BlockSpec((1, tf, D_MODEL), lambda i, j, sr, te, na: (te[row(i, na)], jj(i, j, na), 0)),
            ],
            out_specs=pl.BlockSpec((te_rows, ROW_CHUNKS, LANES), lambda i, j, sr, te, na: (i, 0, 0)),
            scratch_shapes=[
                pltpu.VMEM((2, te_rows, ROW_CHUNKS, LANES), F32),
                pltpu.VMEM((te_rows, D_MODEL), BF16),
                pltpu.VMEM((ROW_CHUNKS * _cm_stride(te_rows), LANES), F32),
                pltpu.SemaphoreType.DMA((2,)),
            ],
        ),
        out_shape=jax.ShapeDtypeStruct((s_max, ROW_CHUNKS, LANES), F32),
        compiler_params=_cparams(("arbitrary", "arbitrary")),
        name="moe_experts",
    )(src, tile_expert, n_active, x3, g, wg, wu, wd)


def _combine_kernel(back_ref, ys3_ref, x_ref, meta_ref, g_ref, y_ref, buf_ref, cm_ref, sem, *, tile0, n_tok):
    i = pl.program_id(0)
    nt = pl.num_programs(0)
    tm = x_ref.shape[0]
    stride = cm_ref.shape[0] // ROW_CHUNKS
    slot = i % 2

    def gather(tile, slt, wait):
        for kk in range(TOP_K):
            _gather_tile(back_ref, kk * n_tok + (tile0 + tile) * tm, tm, ys3_ref, buf_ref.at[slt], kk * tm,
                         sem.at[slt], wait=wait)

    @pl.when(i == 0)
    def _():
        gather(0, 0, False)

    @pl.when(i + 1 < nt)
    def _():
        gather(i + 1, 1 - slot, False)

    gather(i, slot, True)
    _rows_to_chunk_major(buf_ref.at[slot], cm_ref, TOP_K * tm, stride)

    meta = meta_ref[...]
    p1 = meta[:, 2:3]
    p2 = meta[:, 3:4]
    ss = jnp.zeros((tm, 1), F32)
    for c in range(ROW_CHUNKS):
        csl = slice(c * LANES, (c + 1) * LANES)
        e1 = cm_ref[pl.ds(c * stride, tm), :]
        e2 = cm_ref[pl.ds(c * stride + tm, tm), :]
        xn = x_ref[:, csl] + (p1 * e1 + p2 * e2)
        ss = ss + jnp.sum(xn * xn, axis=-1, keepdims=True)
        y_ref[:, csl] = xn
    y_ref[...] = y_ref[...] * lax.rsqrt(ss / D_MODEL + RMS_EPS) * g_ref[...]


def _combine(x, ys3, back, meta, g, row0, rows):
    n = x.shape[0]
    tm = _tile(math.gcd(math.gcd(row0, rows), n), 256)
    b0 = row0 // tm
    kern = functools.partial(_combine_kernel, tile0=b0, n_tok=n)
    return pl.pallas_call(
        kern,
        grid_spec=pltpu.PrefetchScalarGridSpec(
            num_scalar_prefetch=1,
            grid=(rows // tm,),
            in_specs=[
                pl.BlockSpec(memory_space=pl.ANY),
                pl.BlockSpec((tm, D_MODEL), lambda i, bk: (b0 + i, 0)),
                pl.BlockSpec((tm, LANES), lambda i, bk: (b0 + i, 0)),
                pl.BlockSpec((1, D_MODEL), lambda i, bk: (0, 0)),
            ],
            out_specs=pl.BlockSpec((tm, D_MODEL), lambda i, bk: (i, 0)),
            scratch_shapes=[
                pltpu.VMEM((2, TOP_K * tm, ROW_CHUNKS, LANES), F32),
                pltpu.VMEM((ROW_CHUNKS * _cm_stride(TOP_K * tm), LANES), F32),
                pltpu.SemaphoreType.DMA((2,)),
            ],
        ),
        out_shape=jax.ShapeDtypeStruct((rows, D_MODEL), F32),
        compiler_params=_cparams(("arbitrary",), SMALL_VMEM_LIMIT),
        name="moe_combine",
    )(back, ys3, x, meta, g)


EXPERT_TILE = 512


def _routing_tables(meta, counts, n):
    te = EXPERT_TILE
    i1 = meta[:, 0].astype(jnp.int32)
    i2 = meta[:, 1].astype(jnp.int32)
    r1 = meta[:, 4].astype(jnp.int32)
    r2 = meta[:, 5].astype(jnp.int32)
    cnt = counts[0, :N_EXPERTS].astype(jnp.int32)
    padded = (cnt + te - 1) // te * te
    ends = jnp.cumsum(padded)
    offs = ends - padded
    slot1 = offs[i1] + r1
    slot2 = offs[i2] + r2
    s_max = (TOP_K * n + N_EXPERTS * (te - 1) + te - 1) // te * te
    tok = jnp.arange(n, dtype=jnp.int32)
    src = jnp.zeros((s_max,), jnp.int32).at[slot1].set(tok).at[slot2].set(tok)
    n_active = (ends[-1] // te).astype(jnp.int32).reshape(1)
    starts = jnp.arange(s_max // te, dtype=jnp.int32) * te
    tile_expert = jnp.minimum(jnp.searchsorted(ends, starts, side="right"), N_EXPERTS - 1).astype(jnp.int32)
    return src, jnp.concatenate([slot1, slot2]), tile_expert, n_active


def _prep_decay_params(a_log, dt_bias):
    rows = jnp.zeros((SUBLANES, LANES), F32)
    rows = rows.at[0, DN_HEADS:2 * DN_HEADS].set(a_log)
    rows = rows.at[1, DN_HEADS:2 * DN_HEADS].set(dt_bias)
    return rows


def _prep_gmlp_params(sp_w, sp_b, sample_len):
    reps = GMLP_CHUNK // sample_len
    w_s = jnp.tile(sp_w[:, :sample_len, :sample_len], (1, reps, reps))
    b_s = jnp.tile(sp_b[:, :sample_len], (1, reps))
    w2 = jnp.stack([sp_w, w_s])
    b2 = jnp.stack([sp_b.T, b_s.T])
    b2 = jnp.pad(b2, ((0, 0), (0, 0), (0, LANES - GMLP_GROUPS)))
    return w2, b2


def kernel(x_prompt, x_sample, state_conv, state_delta, norm_mix_g, w_in, conv_w, a_log, dt_bias, dn_norm_g,
           gm_ln_g, gm_ln_b, sp_w, sp_b, w_dn_out, w_gm_out, w_out, norm_ffn_g, ffn_wg, ffn_wu, ffn_wd,
           router_w, router_b, moe_wg, moe_wu, moe_wd, final_g):
    bp, tp, _ = x_prompt.shape
    bs, ts, _ = x_sample.shape
    n_p, n_s = bp * tp, bs * ts
    n = n_p + n_s
    assert tp % GMLP_CHUNK == 0 and GMLP_CHUNK % ts == 0 and n_s % GMLP_CHUNK == 0 and ts % CHUNK == 0

    x_parts = [x_prompt.reshape(n_p, D_MODEL), x_sample.reshape(n_s, D_MODEL)]
    w_in_t = jnp.swapaxes(w_in, 1, 2)
    conv_p, conv_s, delta_p, delta_s, v_s = [], [], [], [], []
    y_p = y_s = None
    for l in range(DEPTH):
        w_ba_t = jnp.pad(w_in_t[l, OFF_B:OFF_U, :], ((0, LANES - BA_W), (0, 0)))
        hn, p_ba = _mixnorm(x_parts, norm_mix_g[l].reshape(1, D_MODEL), w_ba_t)
        p_main = _inproj(hn, w_in_t, l)
        u, wq, lk, eg, tq, tk, tv = _delta_pre(p_main, p_ba, state_conv, l, conv_w[l],
                                               _prep_decay_params(a_log[l], dt_bias[l]), bp, tp, bs, ts)
        conv_new = jnp.concatenate([tq, tk, tv], axis=-1)[:, SUBLANES - (CONV_W - 1):, :]
        o, sp_out, ss_out = _delta_rec(u, wq, lk, eg, p_main, state_delta, l, dn_norm_g[l].reshape(1, DN_DV),
                                       bp, tp, bs, ts)
        w2, b2 = _prep_gmlp_params(sp_w[l], sp_b[l], ts)
        gm, vn_s = _gmlp(p_main, gm_ln_g[l].reshape(1, GMLP_W), gm_ln_b[l].reshape(1, GMLP_W), w2, b2, n_p, ts)
        y = _merge(o, gm, w_dn_out, w_gm_out, l, p_main)
        x = _outproj(y, w_out, l, x_parts)

        conv_p.append(conv_new[:bp])
        conv_s.append(conv_new[bp:])
        delta_p.append(sp_out)
        delta_s.append(ss_out)
        v_s.append(vn_s.reshape(bs, ts, GMLP_W))

        g_ffn = norm_ffn_g[l].reshape(1, D_MODEL)
        if l % 2 == 0:
            x = _ffn(x, g_ffn, ffn_wg[l // 2].astype(BF16), ffn_wu[l // 2].astype(BF16), ffn_wd[l // 2].astype(BF16))
            x_parts = [x]
        else:
            e = l // 2
            rw = jnp.pad(router_w[e], ((0, 0), (0, LANES - N_EXPERTS)))
            rb = jnp.pad(router_b[e], (0, LANES - N_EXPERTS), constant_values=-jnp.inf).reshape(1, LANES)
            meta, counts = _router(x, g_ffn, rw, rb)
            src, back, tile_expert, n_active = _routing_tables(meta, counts, n)
            ys3 = _experts(x.reshape(n, ROW_CHUNKS, LANES), src, g_ffn, moe_wg[e], moe_wu[e], moe_wd[e],
                           tile_expert, n_active, EXPERT_TILE)
            if l == DEPTH - 1:
                fg = final_g.reshape(1, D_MODEL)
                y_p = _combine(x, ys3, back, meta, fg, 0, n_p)
                y_s = _combine(x, ys3, back, meta, fg, n_p, n_s)
    return (y_p.reshape(bp, tp, D_MODEL), y_s.reshape(bs, ts, D_MODEL), jnp.stack(conv_p), jnp.stack(delta_p),
            jnp.stack(conv_s), jnp.stack(delta_s), jnp.stack(v_s))
```
